```python
import math
import jax
import jax.numpy as jnp
from jax import lax
import numpy as np

D_MODEL = 1024
BATCH = 1
SEQ = 16384
DEPTH = 2

F32 = jnp.float32

RWKV_HEAD_DIM = 64
RWKV_WIDTH = D_MODEL // 4
RWKV_HEADS = RWKV_WIDTH // RWKV_HEAD_DIM
RWKV_DECAY_RANK = 64
RWKV_ICLR_RANK = 64
RWKV_VRES_RANK = 32
RWKV_GATE_RANK = 128
RWKV_GN_EPS = 64e-5
GDN_HEAD_DIM = 128
GDN_WIDTH = D_MODEL // 2
GDN_HEADS = GDN_WIDTH // GDN_HEAD_DIM
GDN_CONV = 4
GDN_CHUNK = 64
RET_HEAD_DIM = 64
RET_WIDTH = D_MODEL // 4
RET_HEADS = RET_WIDTH // RET_HEAD_DIM
RET_CHUNK = 128
ROPE_BASE = 10000.0

MIX_WIDTH = RWKV_WIDTH + GDN_WIDTH + RET_WIDTH
A_COLS = (RWKV_WIDTH, RWKV_WIDTH, RWKV_WIDTH, RWKV_DECAY_RANK, RWKV_ICLR_RANK, RWKV_GATE_RANK)
B_COLS = (GDN_WIDTH, GDN_WIDTH, GDN_WIDTH, GDN_WIDTH, GDN_HEADS, GDN_HEADS)
C_COLS = (RET_WIDTH, RET_WIDTH, RET_WIDTH, RET_WIDTH)
A_SIZE = sum(A_COLS)
B_SIZE = sum(B_COLS)
C_SIZE = sum(C_COLS)
N_IN = A_SIZE + B_SIZE + C_SIZE

N_EXPERTS = 32
TOP_K = 4
D_FF = D_MODEL
SWIGLU_LIMIT = 7.0
SWIGLU_ALPHA = 1.702
MOE_BLOCK = 128

DEEPNORM_ALPHA = (2 * DEPTH) ** 0.25
DEEPNORM_BETA = (8 * DEPTH) ** -0.25
LN_EPS = 1e-5
RMS_EPS = 1e-6

kernel_name = 'hybrid_rwkv7_gdn_retention_moe_deepnorm'


def _split(t, sizes):
    return jnp.split(t, np.cumsum(sizes)[:-1].tolist(), axis=-1)


def _heads(t, n_heads):
    return t.reshape(t.shape[:-1] + (n_heads, t.shape[-1] // n_heads))


def _merge(t):
    return t.reshape(t.shape[:-2] + (t.shape[-2] * t.shape[-1],))


def _layer_norm(x, g, b):
    xf = x.astype(F32)
    mu = jnp.mean(xf, -1, keepdims=True)
    var = jnp.mean(jnp.square(xf - mu), -1, keepdims=True)
    return ((xf - mu) * lax.rsqrt(var + LN_EPS)).astype(x.dtype) * g + b


def _l2norm(t):
    tf = t.astype(F32)
    return tf * lax.rsqrt(jnp.sum(tf * tf, -1, keepdims=True) + 1e-12)


def _group_norm(t, g, b, eps):
    mu = jnp.mean(t, -1, keepdims=True)
    var = jnp.mean(jnp.square(t - mu), -1, keepdims=True)
    return _merge((t - mu) * lax.rsqrt(var + eps)) * g + b


def _rms_head(t, g):
    return _merge(t * lax.rsqrt(jnp.mean(t * t, -1, keepdims=True) + RMS_EPS) * g)


def _token_shift(p):
    return jnp.pad(p, ((0, 0), (1, 0), (0, 0)))[:, :-1]


def _causal_depthwise_conv(x, w):
    k_size, ch = w.shape
    return lax.conv_general_dilated(x, w[:, None, :], window_strides=(1,), padding=[(k_size - 1, 0)],
                                    dimension_numbers=('NWC', 'WIO', 'NWC'), feature_group_count=ch)


def _rotary(t, positions):
    half = t.shape[-1] // 2
    inv_freq = ROPE_BASE ** (-jnp.arange(half, dtype=F32) / half)
    ang = positions.astype(F32)[..., None] * inv_freq
    cos, sin = jnp.cos(ang)[:, :, None, :], jnp.sin(ang)[:, :, None, :]
    t1, t2 = t[..., :half], t[..., half:]
    return jnp.concatenate([t1 * cos - t2 * sin, t1 * sin + t2 * cos], axis=-1)


def _rwkv7_scan(r, w, k, v, a_vec, b_vec):
    bsz, _, n_heads, n = r.shape

    def step(state, inp):
        r_t, w_t, k_t, v_t, a_t, b_t = inp
        sa = jnp.einsum('bhij,bhj->bhi', state, a_t)
        state = (state * w_t[:, :, None, :] + sa[..., None] * b_t[:, :, None, :]
                 + v_t[..., None] * k_t[:, :, None, :])
        return state, jnp.einsum('bhij,bhj->bhi', state, r_t)

    xs = tuple(jnp.moveaxis(t, 1, 0) for t in (r, w, k, v, a_vec, b_vec))
    _, ys = lax.scan(step, jnp.zeros((bsz, n_heads, n, n), F32), xs)
    return jnp.moveaxis(ys, 0, 1)


def _rwkv7_group(pieces, v_lo, v_first, w0, w2, a0, a2, g2, k_k, k_a, r_k, ln_g, ln_b, v0, v2):
    r, k, v, w_lo, a_lo, g_lo = pieces
    w = -jax.nn.softplus(-(w0 + jnp.tanh(w_lo) @ w2)) - 0.5
    decay = jnp.exp(-jnp.exp(w.astype(F32)))
    a = jax.nn.sigmoid(a0 + a_lo @ a2)
    g = jax.nn.sigmoid(g_lo) @ g2
    if v_first is None:
        v_first = v
    else:
        v = v + (v_first - v) * jax.nn.sigmoid(v0 + v_lo @ v2)
    kk = _l2norm(_heads(k * k_k, RWKV_HEADS))
    k = k * (1.0 + (a - 1.0) * k_a)
    rh, kh, vh, ah, dh = (_heads(t, RWKV_HEADS).astype(F32) for t in (r, k, v, a, decay))
    y = _rwkv7_scan(rh, dh, kh, vh, -kk, kk * ah)
    y = _group_norm(y, ln_g, ln_b, RWKV_GN_EPS)
    bonus = _merge(jnp.sum(rh * kh * r_k, -1, keepdims=True) * vh)
    return (y + bonus) * g, v_first


def _gated_delta_chunked(q, k, v, g, beta):
    bsz, seq, n_heads, dk = q.shape
    dv = v.shape[-1]
    n_chunks = seq // GDN_CHUNK

    def chunks(t):
        return jnp.moveaxis(t.reshape((bsz, n_chunks, GDN_CHUNK, n_heads) + t.shape[3:]), 3, 1)

    q, k, v, g, beta = (chunks(t) for t in (q, k, v, g, beta))
    gc = jnp.cumsum(g, axis=-1)
    idx = jnp.arange(GDN_CHUNK)
    causal = idx[:, None] >= idx[None, :]
    strict = idx[:, None] > idx[None, :]
    decay_mask = jnp.exp(jnp.where(causal, gc[..., :, None] - gc[..., None, :], -jnp.inf))
    kb = k * beta[..., None]
    m_low = jnp.where(strict, jnp.einsum('bhncd,bhnmd->bhncm', kb, k) * decay_mask, 0.0)
    rhs = jnp.concatenate([v * beta[..., None], kb * jnp.exp(gc)[..., None]], axis=-1)
    sol = lax.linalg.triangular_solve(m_low, rhs, left_side=True, lower=True, unit_diagonal=True)
    u, w = sol[..., :dv], sol[..., dv:]
    attn = jnp.einsum('bhncd,bhnmd->bhncm', q, k) * decay_mask
    q_dec = q * jnp.exp(gc)[..., None]
    g_last = gc[..., -1]
    k_dec = k * jnp.exp(g_last[..., None] - gc)[..., None]

    def step(state, inp):
        u_n, w_n, attn_n, qd_n, kd_n, gl_n = inp
        v_new = u_n - jnp.einsum('bhcd,bhde->bhce', w_n, state)
        o = jnp.einsum('bhcd,bhde->bhce', qd_n, state) + jnp.einsum('bhcm,bhme->bhce', attn_n, v_new)
        state = state * jnp.exp(gl_n)[..., None, None] + jnp.einsum('bhcd,bhce->bhde', kd_n, v_new)
        return state, o

    xs = tuple(jnp.moveaxis(t, 2, 0) for t in (u, w, attn, q_dec, k_dec, g_last))
    _, o = lax.scan(step, jnp.zeros((bsz, n_heads, dk, dv), F32), xs)
    o = jnp.moveaxis(jnp.moveaxis(o, 0, 2), 1, 3)
    return o.reshape(bsz, seq, n_heads, dv)


def _gdn_group(pieces, conv_w, a_log, dt_bias, norm_g):
    q, k, v, z, a, b = pieces
    qkv = jax.nn.silu(_causal_depthwise_conv(jnp.concatenate([q, k, v], axis=-1), conv_w))
    q, k, v = jnp.split(qkv, 3, axis=-1)
    qh = _l2norm(_heads(q, GDN_HEADS)) * GDN_HEAD_DIM ** -0.5
    kh = _l2norm(_heads(k, GDN_HEADS))
    vh = _heads(v, GDN_HEADS).astype(F32)
    beta = jax.nn.sigmoid(b.astype(F32))
    g = -jnp.exp(a_log.astype(F32)) * jax.nn.softplus(a.astype(F32) + dt_bias.astype(F32))
    o = _gated_delta_chunked(qh, kh, vh, g, beta)
    return _rms_head(o, norm_g) * jax.nn.silu(z)


def _retention_chunked(q, k, v, log_gamma):
    bsz, seq, n_heads, _ = q.shape
    dv = v.shape[-1]
    n_chunks = seq // RET_CHUNK

    def chunks(t):
        return jnp.moveaxis(t.reshape(bsz, n_chunks, RET_CHUNK, n_heads, t.shape[-1]), 3, 1)

    q, k, v = chunks(q), chunks(k), chunks(v)
    idx = jnp.arange(RET_CHUNK, dtype=F32)
    dist = idx[:, None] - idx[None, :]
    dmat = jnp.exp(jnp.where(dist >= 0, dist * log_gamma[:, None, None], -jnp.inf))
    scores = jnp.einsum('bhncd,bhnmd->bhncm', q, k) * dmat[None, :, None]
    inner = jnp.einsum('bhncm,bhnme->bhnce', scores, v)
    zeta = jnp.exp((RET_CHUNK - 1.0 - idx)[None] * log_gamma[:, None])
    xi = jnp.exp((idx + 1.0)[None] * log_gamma[:, None])
    kv = jnp.einsum('bhncd,bhnce->bhnde', k * zeta[None, :, None, :, None], v)
    chunk_decay = jnp.exp(RET_CHUNK * log_gamma)[None, :, None, None]

    def step(state, kv_n):
        return state * chunk_decay + kv_n, state

    _, r_prev = lax.scan(step, jnp.zeros(kv.shape[:2] + kv.shape[3:], F32), jnp.moveaxis(kv, 2, 0))
    r_prev = jnp.moveaxis(r_prev, 0, 2)
    cross = jnp.einsum('bhncd,bhnde->bhnce', q * xi[None, :, None, :, None], r_prev)
    o = jnp.moveaxis(inner + cross, 1, 3)
    return o.reshape(bsz, seq, n_heads, dv)


def _retention_group(pieces, positions, ln_g, ln_b):
    q, k, v, g = pieces
    qh = _rotary(_heads(q, RET_HEADS).astype(F32), positions)
    kh = _rotary(_heads(k, RET_HEADS).astype(F32), positions) * RET_HEAD_DIM ** -0.5
    vh = _heads(v, RET_HEADS).astype(F32)
    log_gamma = jnp.log1p(-jnp.exp2(-5.0 - jnp.arange(RET_HEADS, dtype=F32)))
    o = _retention_chunked(qh, kh, vh, log_gamma)
    return _group_norm(o, ln_g, ln_b, LN_EPS) * jax.nn.silu(g)


def _clamped_swiglu(up):
    glu = jnp.minimum(up[..., :D_FF], SWIGLU_LIMIT)
    lin = jnp.clip(up[..., D_FF:], -SWIGLU_LIMIT, SWIGLU_LIMIT)
    return glu * jax.nn.sigmoid(SWIGLU_ALPHA * glu) * (lin + 1.0)


def _moe(h, router_w, router_b, w_up, b_up, w_down, b_down):
    bsz, seq, dm = h.shape
    n_tok = bsz * seq
    hf = h.reshape(n_tok, dm)
    logits = hf.astype(F32) @ router_w.astype(F32) + router_b.astype(F32)
    top_val, top_idx = lax.top_k(logits, TOP_K)
    gates = jax.nn.softmax(top_val, axis=-1)
    n_assign = n_tok * TOP_K
    flat_e = top_idx.reshape(n_assign)
    flat_tok = jnp.repeat(jnp.arange(n_tok, dtype=jnp.int32), TOP_K)
    order = jnp.argsort(flat_e)
    e_s, tok_s, g_s = flat_e[order], flat_tok[order], gates.reshape(n_assign)[order]
    counts = jax.ops.segment_sum(jnp.ones((n_assign,), jnp.int32), flat_e, num_segments=N_EXPERTS)
    padded = (counts + MOE_BLOCK - 1) // MOE_BLOCK * MOE_BLOCK
    start = jnp.cumsum(counts) - counts
    pend = jnp.cumsum(padded)
    pstart = pend - padded
    dest = pstart[e_s] + jnp.arange(n_assign, dtype=jnp.int32) - start[e_s]
    n_blocks = -(-(n_assign + N_EXPERTS * (MOE_BLOCK - 1)) // MOE_BLOCK)
    cap = n_blocks * MOE_BLOCK
    pad_tok = jnp.zeros((cap,), jnp.int32).at[dest].set(tok_s)
    pad_g = jnp.zeros((cap,), h.dtype).at[dest].set(g_s.astype(h.dtype))
    block_e = jnp.minimum(jnp.searchsorted(pend, jnp.arange(n_blocks) * MOE_BLOCK, side='right'),
                          N_EXPERTS - 1).astype(jnp.int32)

    def expert_block(args):
        tok_b, g_b, e = args
        up = hf[tok_b] @ w_up[e] + b_up[e]
        y = _clamped_swiglu(up) @ w_down[e] + b_down[e]
        return y * g_b[:, None]

    ys = lax.map(expert_block, (pad_tok.reshape(n_blocks, MOE_BLOCK), pad_g.reshape(n_blocks, MOE_BLOCK), block_e))
    out = jnp.zeros((n_tok, dm), h.dtype).at[pad_tok].add(ys.reshape(cap, dm))
    return out.reshape(bsz, seq, dm)


def setup_inputs(seed: int = 0) -> dict:
    key = jax.random.key(seed)
    ks = iter(jax.random.split(key, 64))

    def nrm(shape, scale):
        return jax.random.normal(next(ks), shape, F32) * scale

    def unif(shape, lo, hi):
        return jax.random.uniform(next(ks), shape, F32, minval=lo, maxval=hi)

    d, L = D_MODEL, DEPTH
    x = nrm((BATCH, SEQ, d), 1.0)
    c = nrm((BATCH, d), 1.0)
    offset = jax.random.randint(next(ks), (BATCH, 1), 0, 4096, dtype=jnp.int32)
    positions = offset + jnp.arange(SEQ, dtype=jnp.int32)[None, :]
    n = jnp.arange(RWKV_WIDTH, dtype=F32) / (RWKV_WIDTH - 1)
    w0_base = -7.0 + 5.0 * n ** 0.85 + 0.5
    dt = jnp.exp(unif((L, GDN_HEADS), math.log(1e-3), math.log(1e-1)))
    return {
        'x': x,
        'c': c,
        'positions': positions,
        'ada_w': nrm((L, d, 6 * d), 0.1 * d ** -0.5),
        'ada_b': nrm((L, 6 * d), 0.02),
        'w_in': nrm((L, d, N_IN), d ** -0.5),
        'w_in_vres': nrm((L - 1, d, RWKV_VRES_RANK), d ** -0.5),
        'tshift_mu': unif((L, A_SIZE), 0.0, 1.0),
        'tshift_mu_vres': unif((L - 1, RWKV_VRES_RANK), 0.0, 1.0),
        'rwkv_w0': w0_base[None] + nrm((L, RWKV_WIDTH), 0.1),
        'rwkv_w2': nrm((L, RWKV_DECAY_RANK, RWKV_WIDTH), 0.1 * RWKV_DECAY_RANK ** -0.5),
        'rwkv_a0': nrm((L, RWKV_WIDTH), 0.1),
        'rwkv_a2': nrm((L, RWKV_ICLR_RANK, RWKV_WIDTH), 0.1 * RWKV_ICLR_RANK ** -0.5),
        'rwkv_g2': nrm((L, RWKV_GATE_RANK, RWKV_WIDTH), RWKV_GATE_RANK ** -0.5),
        'rwkv_kk': 0.85 + nrm((L, RWKV_WIDTH), 0.02),
        'rwkv_ka': 1.0 + nrm((L, RWKV_WIDTH), 0.02),
        'rwkv_rk': nrm((L, RWKV_HEADS, RWKV_HEAD_DIM), 0.1),
        'rwkv_ln_g': 1.0 + nrm((L, RWKV_WIDTH), 0.02),
        'rwkv_ln_b': nrm((L, RWKV_WIDTH), 0.02),
        'rwkv_v0': 1.0 + nrm((L - 1, RWKV_WIDTH), 0.1),
        'rwkv_v2': nrm((L - 1, RWKV_VRES_RANK, RWKV_WIDTH), 0.1 * RWKV_VRES_RANK ** -0.5),
        'gdn_conv_w': nrm((L, GDN_CONV, 3 * GDN_WIDTH), GDN_CONV ** -0.5),
        'gdn_a_log': jnp.log(unif((L, GDN_HEADS), 1.0, 16.0)),
        'gdn_dt_bias': dt + jnp.log(-jnp.expm1(-dt)),
        'gdn_norm_g': 1.0 + nrm((L, GDN_HEAD_DIM), 0.02),
        'ret_norm_g': 1.0 + nrm((L, RET_WIDTH), 0.02),
        'ret_norm_b': nrm((L, RET_WIDTH), 0.02),
        'w_out': nrm((L, MIX_WIDTH, d), MIX_WIDTH ** -0.5 * DEEPNORM_BETA),
        'ln1_g': 1.0 + nrm((L, d), 0.02),
        'ln1_b': nrm((L, d), 0.02),
        'router_w': nrm((L, d, N_EXPERTS), d ** -0.5),
        'router_b': nrm((L, N_EXPERTS), 0.01),
        'exp_w_up': nrm((L, N_EXPERTS, d, 2 * D_FF), d ** -0.5),
        'exp_b_up': nrm((L, N_EXPERTS, 2 * D_FF), 0.02),
        'exp_w_down': nrm((L, N_EXPERTS, D_FF, d), D_FF ** -0.5 * DEEPNORM_BETA),
        'exp_b_down': nrm((L, N_EXPERTS, d), 0.02),
        'ln2_g': 1.0 + nrm((L, d), 0.02),
        'ln2_b': nrm((L, d), 0.02),
    }


def reference(x, c, positions, ada_w, ada_b, w_in, w_in_vres, tshift_mu, tshift_mu_vres,
              rwkv_w0, rwkv_w2, rwkv_a0, rwkv_a2, rwkv_g2, rwkv_kk, rwkv_ka, rwkv_rk,
              rwkv_ln_g, rwkv_ln_b, rwkv_v0, rwkv_v2, gdn_conv_w, gdn_a_log, gdn_dt_bias,
              gdn_norm_g, ret_norm_g, ret_norm_b, w_out, ln1_g, ln1_b, router_w, router_b,
              exp_w_up, exp_b_up, exp_w_down, exp_b_down, ln2_g, ln2_b):
    cond = jax.nn.silu(c)
    v_first = None
    for l in range(DEPTH):
        ada = (cond @ ada_w[l] + ada_b[l])[:, None, :]
        sh_mix, sc_mix, gt_mix, sh_ffn, sc_ffn, gt_ffn = jnp.split(ada, 6, axis=-1)

        h = x * (1.0 + sc_mix) + sh_mix
        if l == 0:
            w_comb, mu = w_in[l], tshift_mu[l]
            v0, v2 = None, None
        else:
            w_comb = jnp.concatenate([w_in[l], w_in_vres[l - 1]], axis=1)
            mu = jnp.concatenate([tshift_mu[l], tshift_mu_vres[l - 1]])
            v0, v2 = rwkv_v0[l - 1], rwkv_v2[l - 1]
        proj = h @ w_comb
        p_shift = jnp.concatenate([proj[..., :A_SIZE], proj[..., N_IN:]], axis=-1)
        p_shift = p_shift + (_token_shift(p_shift) - p_shift) * mu
        y_a, v_first = _rwkv7_group(_split(p_shift[..., :A_SIZE], A_COLS), p_shift[..., A_SIZE:], v_first,
                                    rwkv_w0[l], rwkv_w2[l], rwkv_a0[l], rwkv_a2[l], rwkv_g2[l],
                                    rwkv_kk[l], rwkv_ka[l], rwkv_rk[l], rwkv_ln_g[l], rwkv_ln_b[l], v0, v2)
        y_b = _gdn_group(_split(proj[..., A_SIZE:A_SIZE + B_SIZE], B_COLS),
                         gdn_conv_w[l], gdn_a_log[l], gdn_dt_bias[l], gdn_norm_g[l])
        y_c = _retention_group(_split(proj[..., A_SIZE + B_SIZE:N_IN], C_COLS), positions,
                               ret_norm_g[l], ret_norm_b[l])
        mix = jnp.concatenate([y_a, y_b, y_c], axis=-1).astype(x.dtype) @ w_out[l]
        x = _layer_norm(DEEPNORM_ALPHA * x + (1.0 + gt_mix) * mix, ln1_g[l], ln1_b[l])

        h = x * (1.0 + sc_ffn) + sh_ffn
        ffn = _moe(h, router_w[l], router_b[l], exp_w_up[l], exp_b_up[l], exp_w_down[l], exp_b_down[l])
        x = _layer_norm(DEEPNORM_ALPHA * x + (1.0 + gt_ffn) * ffn, ln2_g[l], ln2_b[l])
    return x
```

```python
import functools
import math

import numpy as np
import jax
import jax.numpy as jnp
from jax import lax
from jax.experimental import pallas as pl
from jax.experimental.pallas import tpu as pltpu

F32 = jnp.float32
BF16 = jnp.bfloat16
I32 = jnp.int32

D_MODEL = 1024
RWKV_HEADS, RWKV_HD, RWKV_W = 4, 64, 256
RWKV_GN_EPS = 64e-5
GDN_HEADS, GDN_HD, GDN_W = 4, 128, 512
GDN_CONV = 4
RET_HEADS, RET_HD, RET_W = 4, 64, 256
RET_CHUNK = 128
ROPE_BASE = 10000.0
N_EXPERTS, TOP_K, D_FF = 32, 4, 1024
SWIGLU_LIMIT, SWIGLU_ALPHA = 7.0, 1.702
LN_EPS, RMS_EPS = 1e-5, 1e-6

CHUNK = 64
STACK = 4 * CHUNK
MIX_TILE = 512
PROJ_TILE = 256
MOE_ROWS = 256
MOE_TILE = 256
NEG_BIG = -1e30
VMEM_LIMIT = 56 * 1024 * 1024

COL_A, COL_B, COL_C, COL_X = 0, 1024, 3072, 4096
N_PROJ = 4224
X_VRES, X_GA, X_GB = 0, 32, 36


def _dot(a, b):
    return jnp.dot(a, b, preferred_element_type=F32)


def _mm(a, b):
    return _dot(a.astype(BF16), b.astype(BF16))


def _mm_nt(a, b):
    return lax.dot_general(a.astype(BF16), b.astype(BF16), (((1,), (1,)), ((), ())),
                           preferred_element_type=F32)


def _split3(x):
    x1 = x.astype(BF16)
    r1 = x - x1.astype(F32)
    x2 = r1.astype(BF16)
    x3 = (r1 - x2.astype(F32)).astype(BF16)
    return x1, x2, x3


def _sel_mm(sel, x):
    x1, x2, x3 = _split3(x)
    return _dot(sel, x1) + (_dot(sel, x2) + _dot(sel, x3))


def _mm_sel(x, sel):
    x1, x2, x3 = _split3(x)
    return _dot(x1, sel) + (_dot(x2, sel) + _dot(x3, sel))


def _sel_mm_nt(sel, x):
    dn = (((1,), (1,)), ((), ()))
    x1, x2, x3 = _split3(x)
    d = lambda a, b: lax.dot_general(a, b, dn, preferred_element_type=F32)
    return d(sel, x1) + (d(sel, x2) + d(sel, x3))


def _mm3(a, b):
    a1 = a.astype(BF16)
    a2 = (a - a1.astype(F32)).astype(BF16)
    b1 = b.astype(BF16)
    b2 = (b - b1.astype(F32)).astype(BF16)
    return _dot(a1, b1) + (_dot(a1, b2) + _dot(a2, b1))


def _mm3_nt(a, b):
    dn = (((1,), (1,)), ((), ()))
    d = lambda x, y: lax.dot_general(x, y, dn, preferred_element_type=F32)
    a1 = a.astype(BF16)
    a2 = (a - a1.astype(F32)).astype(BF16)
    b1 = b.astype(BF16)
    b2 = (b - b1.astype(F32)).astype(BF16)
    return d(a1, b1) + (d(a1, b2) + d(a2, b1))


def _iota2(shape, axis):
    return lax.broadcasted_iota(I32, shape, axis)


def _softplus(x):
    return jnp.maximum(x, 0.0) + jnp.log1p(jnp.exp(-jnp.abs(x)))


def _sigmoid(x):
    return jax.nn.sigmoid(x)


def _silu(x):
    return x * jax.nn.sigmoid(x)


def _same_block(n_rows, n_cols, block):
    r = _iota2((n_rows, n_cols), 0)
    c = _iota2((n_rows, n_cols), 1)
    return (r // block) == (c // block), r, c


def _block_ones(n, block):
    same, _, _ = _same_block(n, n, block)
    return jnp.where(same, 1.0, 0.0).astype(BF16)


def _block_cumsum_mat(n, block):
    same, r, c = _same_block(n, n, block)
    return jnp.where(same & (c <= r), 1.0, 0.0).astype(BF16)


def _head_masks(width, head_dim):
    lane = _iota2((1, width), 1)
    return [jnp.where((lane // head_dim) == h, 1.0, 0.0).astype(F32) for h in range(width // head_dim)]


def _stack_heads(x, masks):
    return jnp.concatenate([x * m for m in masks], axis=0)


def _fold_heads(x4, n_heads):
    c = x4.shape[0] // n_heads
    out = x4[0:c]
    for h in range(1, n_heads):
        out = out + x4[h * c:(h + 1) * c]
    return out


def _unit_lower_inverse(m):
    n = m.shape[0]
    eye = jnp.where(_iota2((n, n), 0) == _iota2((n, n), 1), 1.0, 0.0).astype(F32)
    t = eye - m
    p = m
    for _ in range(5):
        p = _mm3(p, p)
        t = t + _mm3(t, p)
    return t


def _layer_norm_rows(x, g, b):
    mu = jnp.mean(x, axis=-1, keepdims=True)
    xc = x - mu
    var = jnp.mean(xc * xc, axis=-1, keepdims=True)
    return xc * lax.rsqrt(var + LN_EPS) * g + b


def _shift_rows(p, prev_last_row):
    rolled = pltpu.roll(p, 1, axis=0)
    return jnp.where(_iota2(p.shape, 0) == 0, prev_last_row, rolled)


def _params(n_grid_axes=1, vmem=VMEM_LIMIT):
    return pltpu.CompilerParams(dimension_semantics=("arbitrary",) * n_grid_axes, vmem_limit_bytes=vmem)


def _row_spec(tile, width):
    return pl.BlockSpec((tile, width), lambda i: (i, 0))


def _const_spec(shape):
    zeros = (0,) * len(shape)
    return pl.BlockSpec(shape, lambda i: zeros)


def _ada_kernel(c_ref, w_ref, b_ref, o_ref):
    c = c_ref[...]
    o_ref[0] = _mm3(_silu(c), w_ref[0]) + b_ref[0]


def _ada(c, ada_w, ada_b):
    depth, d, d6 = ada_w.shape
    blk = 1024
    c8 = jnp.broadcast_to(c, (8, d))
    out = pl.pallas_call(
        _ada_kernel,
        grid=(depth, d6 // blk),
        in_specs=[pl.BlockSpec((8, d), lambda l, j: (0, 0)),
                  pl.BlockSpec((1, d, blk), lambda l, j: (l, 0, j)),
                  pl.BlockSpec((1, 1, blk), lambda l, j: (l, 0, j))],
        out_specs=pl.BlockSpec((1, 8, blk), lambda l, j: (l, 0, j)),
        out_shape=jax.ShapeDtypeStruct((depth, 8, d6), F32),
        compiler_params=_params(2),
        name="ada",
    )(c8, ada_w, ada_b.reshape(depth, 1, d6))
    return out[:, 0:1, :]


def _inproj_kernel(x_ref, sc_ref, sh_ref, w_ref, mua_ref, mux_ref, pa_ref, pb_ref, pc_ref, px_ref, carry_ref):
    @pl.when(pl.program_id(0) == 0)
    def _():
        carry_ref[...] = jnp.zeros_like(carry_ref)

    tile = x_ref.shape[0]
    h = (x_ref[...] * (1.0 + sc_ref[...]) + sh_ref[...]).astype(BF16)
    pa = _dot(h, w_ref[:, COL_A:COL_B])
    px = _dot(h, w_ref[:, COL_X:N_PROJ])
    pb_ref[...] = _dot(h, w_ref[:, COL_B:COL_C])
    pc_ref[...] = _dot(h, w_ref[:, COL_C:COL_X])
    prev_a = _shift_rows(pa, carry_ref[7:8, 0:1024])
    prev_x = _shift_rows(px, carry_ref[7:8, 1024:1152])
    carry_ref[:, 0:1024] = pa[tile - 8:tile, :]
    carry_ref[:, 1024:1152] = px[tile - 8:tile, :]
    pa_ref[...] = pa + (prev_a - pa) * mua_ref[...]
    px_ref[...] = px + (prev_x - px) * mux_ref[...]


def _inproj(x2, sc, sh, w_cat, mu_a, mu_x):
    n, d = x2.shape
    tile = PROJ_TILE
    return pl.pallas_call(
        _inproj_kernel,
        grid=(n // tile,),
        in_specs=[_row_spec(tile, d), _const_spec((1, d)), _const_spec((1, d)),
                  _const_spec((d, N_PROJ)), _const_spec((1, 1024)), _const_spec((1, 128))],
        out_specs=[_row_spec(tile, 1024), _row_spec(tile, 2048), _row_spec(tile, 1024), _row_spec(tile, 128)],
        out_shape=[jax.ShapeDtypeStruct((n, 1024), F32), jax.ShapeDtypeStruct((n, 2048), F32),
                   jax.ShapeDtypeStruct((n, 1024), F32), jax.ShapeDtypeStruct((n, 128), F32)],
        scratch_shapes=[pltpu.VMEM((8, 1152), F32)],
        compiler_params=_params(),
        name="inproj",
    )(x2, sc, sh, w_cat, mu_a, mu_x)


def _rwkv_kernel(has_vres, *refs):
    if has_vres:
        (pa_ref, px_ref, vf_ref, vec_ref, w2_ref, a2_ref, g2_ref, v2_ref,
         y_ref, state_ref, rt_s, at_s, kt_s, bt_s, kh_s, bh_s, v_s, gc_s, y_s) = refs
    else:
        (pa_ref, vec_ref, w2_ref, a2_ref, g2_ref,
         y_ref, vout_ref, state_ref, rt_s, at_s, kt_s, bt_s, kh_s, bh_s, v_s, gc_s, y_s) = refs

    @pl.when(pl.program_id(0) == 0)
    def _():
        state_ref[...] = jnp.zeros_like(state_ref)

    tile = pa_ref.shape[0]
    w0, a0, k_k, k_a, r_k, ln_g, ln_b, v0 = (vec_ref[i:i + 1, :] for i in range(8))
    r = pa_ref[:, 0:256]
    k = pa_ref[:, 256:512]
    v = pa_ref[:, 512:768]
    wa = pa_ref[:, 768:896]
    g_lo = pa_ref[:, 896:1024]

    w_raw = -_softplus(-(w0 + _mm(jnp.tanh(wa), w2_ref[...]))) - 0.5
    log_decay = -jnp.exp(w_raw)
    a = _sigmoid(a0 + _mm(wa, a2_ref[...]))
    gate = _mm(_sigmoid(g_lo), g2_ref[...])
    if has_vres:
        v = v + (vf_ref[...] - v) * _sigmoid(v0 + _mm(px_ref[...], v2_ref[...]))
    else:
        vout_ref[...] = v

    head_ones = _block_ones(RWKV_W, RWKV_HD)
    kk = k * k_k
    kk = kk * lax.rsqrt(_mm_sel(kk * kk, head_ones) + 1e-12)
    k = k * (1.0 + (a - 1.0) * k_a)
    bonus = _mm_sel(r * k * r_k, head_ones) * v

    cl = _sel_mm(_block_cumsum_mat(tile, CHUNK), log_decay)
    cl_end = _sel_mm(_block_ones(tile, CHUNK), log_decay)
    e_inv = jnp.exp(-cl)
    e_end = jnp.exp(cl_end - cl)
    kka = kk * a
    rt_s[...] = r * jnp.exp(cl)
    at_s[...] = -kk * jnp.exp(cl - log_decay)
    kt_s[...] = k * e_inv
    bt_s[...] = kka * e_inv
    kh_s[...] = k * e_end
    bh_s[...] = kka * e_end
    v_s[...] = v
    gc_s[...] = jnp.exp(cl_end)

    masks = _head_masks(RWKV_W, RWKV_HD)
    same, row, col = _same_block(STACK, STACK, CHUNK)
    strict = same & (row > col)
    incl = same & (row >= col)

    def chunk_body(c, carry):
        rows = pl.ds(pl.multiple_of(c * CHUNK, CHUNK), CHUNK)
        rt4 = _stack_heads(rt_s[rows, :], masks)
        at4 = _stack_heads(at_s[rows, :], masks)
        kt4 = _stack_heads(kt_s[rows, :], masks)
        bt4 = _stack_heads(bt_s[rows, :], masks)
        v4 = _stack_heads(v_s[rows, :], masks)
        kh4 = _stack_heads(kh_s[rows, :], masks)
        bh4 = _stack_heads(bh_s[rows, :], masks)
        a_ak = jnp.where(strict, _mm_nt(at4, kt4), 0.0)
        a_ab = jnp.where(strict, _mm3_nt(at4, bt4), 0.0)
        a_rk = jnp.where(incl, _mm_nt(rt4, kt4), 0.0)
        a_rb = jnp.where(incl, _mm_nt(rt4, bt4), 0.0)
        t_inv = _unit_lower_inverse(-a_ab)
        state = state_ref[...]
        u4 = _mm(t_inv, _mm_nt(at4, state) + _mm(a_ak, v4))
        y4 = _mm_nt(rt4, state) + _mm(a_rk, v4) + _mm(a_rb, u4)
        y_s[rows, :] = _fold_heads(y4, RWKV_HEADS)
        zt = jnp.concatenate([v4, u4], axis=0).T
        kb = jnp.concatenate([kh4, bh4], axis=0)
        state_ref[...] = state * gc_s[pl.ds(c * CHUNK, 1), :] + _mm(zt, kb)
        return carry

    lax.fori_loop(0, tile // CHUNK, chunk_body, 0)

    y = y_s[...]
    inv_n = 1.0 / RWKV_HD
    mu = _mm_sel(y, head_ones) * inv_n
    yc = y - mu
    var = _mm_sel(yc * yc, head_ones) * inv_n
    yn = yc * lax.rsqrt(var + RWKV_GN_EPS) * ln_g + ln_b
    y_ref[...] = (yn + bonus) * gate


def _rwkv(pa, px, v_first, vec, w2p, a2p, g2, v2p):
    n = pa.shape[0]
    tile = MIX_TILE
    has_vres = v_first is not None
    big = [pltpu.VMEM((tile, RWKV_W), F32) for _ in range(9)]
    scratch = [pltpu.VMEM((RWKV_W, RWKV_W), F32)] + big
    wspec = _const_spec((128, RWKV_W))
    if has_vres:
        in_specs = [_row_spec(tile, 1024), _row_spec(tile, 128), _row_spec(tile, RWKV_W),
                    _const_spec((8, RWKV_W)), wspec, wspec, wspec, wspec]
        args = (pa, px, v_first, vec, w2p, a2p, g2, v2p)
        out_specs = _row_spec(tile, RWKV_W)
        out_shape = jax.ShapeDtypeStruct((n, RWKV_W), F32)
    else:
        in_specs = [_row_spec(tile, 1024), _const_spec((8, RWKV_W)), wspec, wspec, wspec]
        args = (pa, vec, w2p, a2p, g2)
        out_specs = [_row_spec(tile, RWKV_W), _row_spec(tile, RWKV_W)]
        out_shape = [jax.ShapeDtypeStruct((n, RWKV_W), F32), jax.ShapeDtypeStruct((n, RWKV_W), F32)]
    return pl.pallas_call(
        functools.partial(_rwkv_kernel, has_vres),
        grid=(n // tile,),
        in_specs=in_specs, out_specs=out_specs, out_shape=out_shape,
        scratch_shapes=scratch,
        compiler_params=_params(),
        name="rwkv7_vres" if has_vres else "rwkv7",
    )(*args)


def _gdn_kernel(pb_ref, px_ref, convw_ref, vec_ref, ea_ref, eb_ref, y_ref,
                state_ref, carry_ref, qd_s, k_s, kb_s, kd_s, rhs_s, gcs_s, gl_s, o_s):
    @pl.when(pl.program_id(0) == 0)
    def _():
        state_ref[...] = jnp.zeros_like(state_ref)
        carry_ref[...] = jnp.zeros_like(carry_ref)

    tile = pb_ref.shape[0]
    width3 = 3 * GDN_W
    a_log, dt_bias, norm_g = (vec_ref[i:i + 1, :] for i in range(3))

    x = pb_ref[:, 0:width3]
    row = _iota2((tile, width3), 0)
    acc = x * convw_ref[GDN_CONV - 1:GDN_CONV, :]
    for d in range(1, GDN_CONV):
        rolled = pltpu.roll(x, d, axis=0)
        prev = carry_ref[...]
        for j in range(d):
            rolled = jnp.where(row == j, prev[8 - d + j:8 - d + j + 1, :], rolled)
        acc = acc + rolled * convw_ref[GDN_CONV - 1 - d:GDN_CONV - d, :]
    carry_ref[...] = x[tile - 8:tile, :]
    qkv = _silu(acc)
    q = qkv[:, 0:GDN_W]
    k = qkv[:, GDN_W:2 * GDN_W]
    v = qkv[:, 2 * GDN_W:width3]
    z = pb_ref[:, width3:width3 + GDN_W]

    head_ones = _block_ones(GDN_W, GDN_HD)
    q = q * lax.rsqrt(_mm_sel(q * q, head_ones) + 1e-12) * (GDN_HD ** -0.5)
    k = k * lax.rsqrt(_mm_sel(k * k, head_ones) + 1e-12)

    px = px_ref[...]
    beta = _sigmoid(_mm_sel(px, eb_ref[...]))
    g = -jnp.exp(a_log) * _softplus(_mm_sel(px, ea_ref[...]) + dt_bias)
    gc = _sel_mm(_block_cumsum_mat(tile, CHUNK), g)
    g_last = _sel_mm(_block_ones(tile, CHUNK), g)
    eg = jnp.exp(gc)
    kb = k * beta
    qd_s[...] = q * eg
    k_s[...] = k
    kb_s[...] = kb
    kd_s[...] = k * jnp.exp(g_last - gc)
    rhs_s[:, 0:GDN_W] = v * beta
    rhs_s[:, GDN_W:2 * GDN_W] = kb * eg
    gcs_s[...] = gc
    gl_s[...] = jnp.exp(g_last)
    o_s[...] = q

    masks = _head_masks(GDN_W, GDN_HD)
    same, rr, cc = _same_block(STACK, STACK, CHUNK)
    strict = same & (rr > cc)
    incl = same & (rr >= cc)
    pick0 = jnp.where(_iota2((STACK, 128), 1) == 0, 1.0, 0.0).astype(BF16)

    def chunk_body(c, carry):
        rows = pl.ds(pl.multiple_of(c * CHUNK, CHUNK), CHUNK)
        q4 = _stack_heads(o_s[rows, :], masks)
        k4 = _stack_heads(k_s[rows, :], masks)
        kb4 = _stack_heads(kb_s[rows, :], masks)
        qd4 = _stack_heads(qd_s[rows, :], masks)
        kd4 = _stack_heads(kd_s[rows, :], masks)
        rhs = rhs_s[rows, :]
        rhs4 = jnp.concatenate([_stack_heads(rhs[:, 0:GDN_W], masks),
                                _stack_heads(rhs[:, GDN_W:2 * GDN_W], masks)], axis=1)
        gcc = gcs_s[rows, :]
        gcol = jnp.concatenate([gcc[:, h * GDN_HD:(h + 1) * GDN_HD] for h in range(GDN_HEADS)], axis=0)
        col_form = jnp.concatenate([gcol, gcol], axis=1)
        row_form = _sel_mm_nt(pick0, gcol)
        decay = jnp.exp(jnp.where(incl, col_form - row_form, NEG_BIG))
        m_low = jnp.where(strict, _mm3_nt(kb4, k4) * decay, 0.0)
        attn = jnp.where(incl, _mm_nt(q4, k4) * decay, 0.0)
        sol = _mm(_unit_lower_inverse(m_low), rhs4)
        u4 = sol[:, 0:GDN_W]
        w4 = sol[:, GDN_W:2 * GDN_W]
        state = state_ref[...]
        v_new = u4 - _mm(w4, state)
        o4 = _mm(qd4, state) + _mm(attn, v_new)
        state_ref[...] = state * gl_s[pl.ds(c * CHUNK, 1), :] + _mm(kd4.T, v_new)
        o_s[rows, :] = _fold_heads(o4, GDN_HEADS)
        return carry

    lax.fori_loop(0, tile // CHUNK, chunk_body, 0)

    o = o_s[...]
    ms = _mm_sel(o * o, head_ones) * (1.0 / GDN_HD)
    y_ref[...] = o * lax.rsqrt(ms + RMS_EPS) * norm_g * _silu(z)


def _gdn(pb, px, conv_w, vec, ea, eb):
    n = pb.shape[0]
    tile = MIX_TILE
    wide = lambda w: pltpu.VMEM((tile, w), F32)
    scratch = [pltpu.VMEM((GDN_W, GDN_W), F32), pltpu.VMEM((8, 3 * GDN_W), F32),
               wide(GDN_W), wide(GDN_W), wide(GDN_W), wide(GDN_W), wide(2 * GDN_W),
               wide(GDN_W), wide(GDN_W), wide(GDN_W)]
    return pl.pallas_call(
        _gdn_kernel,
        grid=(n // tile,),
        in_specs=[_row_spec(tile, 2048), _row_spec(tile, 128), _const_spec((GDN_CONV, 3 * GDN_W)),
                  _const_spec((8, GDN_W)), _const_spec((128, GDN_W)), _const_spec((128, GDN_W))],
        out_specs=_row_spec(tile, GDN_W),
        out_shape=jax.ShapeDtypeStruct((n, GDN_W), F32),
        scratch_shapes=scratch,
        compiler_params=_params(),
        name="gdn",
    )(pb, px, conv_w, vec, ea, eb)


def _ret_kernel(pc_ref, pos_ref, freq_ref, lg_ref, lgs_ref, swap_ref, vec_ref, y_ref,
                state_ref, dmat_ref, xi_ref, zeta_ref, q_s, k_s, v_s, o_s):
    n_stack = RET_HEADS * RET_CHUNK

    @pl.when(pl.program_id(0) == 0)
    def _():
        state_ref[...] = jnp.zeros_like(state_ref)
        same, r, c = _same_block(n_stack, n_stack, RET_CHUNK)
        dist = (r - c).astype(F32)
        dmat_ref[...] = jnp.exp(jnp.where(same & (r >= c), dist * lgs_ref[...], NEG_BIG))
        idx = (_iota2((n_stack, RET_W), 0) % RET_CHUNK).astype(F32)
        xi_ref[...] = jnp.exp((idx + 1.0) * lg_ref[...])
        zeta_ref[...] = jnp.exp((RET_CHUNK - 1.0 - idx) * lg_ref[...])

    tile = pc_ref.shape[0]
    ln_g, ln_b = vec_ref[0:1, :], vec_ref[1:2, :]
    ang = pos_ref[...].astype(F32) * freq_ref[...]
    cos = jnp.cos(ang)
    lane = _iota2((1, RET_W), 1)
    sin_signed = jnp.where((lane % RET_HD) < RET_HD // 2, -1.0, 1.0) * jnp.sin(ang)

    def rotary(t):
        return t * cos + _mm_sel(t, swap_ref[...]) * sin_signed

    q_s[...] = rotary(pc_ref[:, 0:RET_W])
    k_s[...] = rotary(pc_ref[:, RET_W:2 * RET_W]) * (RET_HD ** -0.5)
    v_s[...] = pc_ref[:, 2 * RET_W:3 * RET_W]
    gate = pc_ref[:, 3 * RET_W:4 * RET_W]
    masks = _head_masks(RET_W, RET_HD)
    chunk_decay = jnp.exp(float(RET_CHUNK) * lg_ref[...])

    def chunk_body(c, carry):
        rows = pl.ds(pl.multiple_of(c * RET_CHUNK, RET_CHUNK), RET_CHUNK)
        q4 = _stack_heads(q_s[rows, :], masks)
        k4 = _stack_heads(k_s[rows, :], masks)
        v4 = _stack_heads(v_s[rows, :], masks)
        scores = _mm_nt(q4, k4) * dmat_ref[...]
        state = state_ref[...]
        o4 = _mm(scores, v4) + _mm(q4 * xi_ref[...], state)
        o_s[rows, :] = _fold_heads(o4, RET_HEADS)
        state_ref[...] = state * chunk_decay + _mm((k4 * zeta_ref[...]).T, v4)
        return carry

    lax.fori_loop(0, tile // RET_CHUNK, chunk_body, 0)

    o = o_s[...]
    head_ones = _block_ones(RET_W, RET_HD)
    inv_n = 1.0 / RET_HD
    mu = _mm_sel(o, head_ones) * inv_n
    oc = o - mu
    var = _mm_sel(oc * oc, head_ones) * inv_n
    y_ref[...] = (oc * lax.rsqrt(var + LN_EPS) * ln_g + ln_b) * _silu(gate)


def _ret(pc, pos_col, freq_full, lg_full, lg_stack, swap, vec):
    n = pc.shape[0]
    tile = MIX_TILE
    n_stack = RET_HEADS * RET_CHUNK
    scratch = [pltpu.VMEM((RET_W, RET_W), F32), pltpu.VMEM((n_stack, n_stack), F32),
               pltpu.VMEM((n_stack, RET_W), F32), pltpu.VMEM((n_stack, RET_W), F32)]
    scratch += [pltpu.VMEM((tile, RET_W), F32) for _ in range(4)]
    return pl.pallas_call(
        _ret_kernel,
        grid=(n // tile,),
        in_specs=[_row_spec(tile, 1024), _row_spec(tile, 1), _const_spec((1, RET_W)), _const_spec((1, RET_W)),
                  _const_spec((1, n_stack)), _const_spec((RET_W, RET_W)), _const_spec((8, RET_W))],
        out_specs=_row_spec(tile, RET_W),
        out_shape=jax.ShapeDtypeStruct((n, RET_W), F32),
        scratch_shapes=scratch,
        compiler_params=_params(),
        name="retention",
    )(pc, pos_col, freq_full, lg_full, lg_stack, swap, vec)


def _outproj_kernel(alpha, ya_ref, yb_ref, yc_ref, x_ref, w_ref, gt_ref, g_ref, b_ref, o_ref):
    mix = (_mm(ya_ref[...], w_ref[0:256, :]) + _mm(yb_ref[...], w_ref[256:768, :])
           + _mm(yc_ref[...], w_ref[768:1024, :]))
    o_ref[...] = _layer_norm_rows(alpha * x_ref[...] + (1.0 + gt_ref[...]) * mix, g_ref[...], b_ref[...])


def _outproj(alpha, ya, yb, yc, x2, w_out, gt, ln_g, ln_b):
    n, d = x2.shape
    tile = PROJ_TILE
    vecs = _const_spec((1, d))
    return pl.pallas_call(
        functools.partial(_outproj_kernel, alpha),
        grid=(n // tile,),
        in_specs=[_row_spec(tile, 256), _row_spec(tile, 512), _row_spec(tile, 256), _row_spec(tile, d),
                  _const_spec((d, d)), vecs, vecs, vecs],
        out_specs=_row_spec(tile, d),
        out_shape=jax.ShapeDtypeStruct((n, d), F32),
        compiler_params=_params(),
        name="outproj_ln",
    )(ya, yb, yc, x2, w_out, gt, ln_g, ln_b)


def _router_kernel(x_ref, sc_ref, sh_ref, wt_ref, b_ref, idx_ref, rank_ref, gate_ref, cnt_ref, run_ref):
    @pl.when(pl.program_id(0) == 0)
    def _():
        run_ref[...] = jnp.zeros_like(run_ref)

    tile = x_ref.shape[0]
    h = x_ref[...] * (1.0 + sc_ref[...]) + sh_ref[...]
    logits = _mm3_nt(wt_ref[...], h) + b_ref[:, 0:1]
    e_iota = _iota2((N_EXPERTS, tile), 0).astype(F32)
    vals, hots, idxs = [], [], []
    for _ in range(TOP_K):
        m = jnp.max(logits, axis=0, keepdims=True)
        idx = jnp.min(jnp.where(logits == m, e_iota, float(N_EXPERTS)), axis=0, keepdims=True)
        hot = e_iota == idx
        logits = jnp.where(hot, -jnp.inf, logits)
        vals.append(m)
        idxs.append(idx.astype(I32))
        hots.append(hot)
    exps = [jnp.exp(vk - vals[0]) for vk in vals]
    denom = exps[0] + exps[1] + exps[2] + exps[3]
    hot_all = jnp.where(hots[0] | hots[1] | hots[2] | hots[3], 1.0, 0.0)
    before = jnp.where(_iota2((tile, tile), 0) < _iota2((tile, tile), 1), 1.0, 0.0).astype(BF16)
    seen = _dot(hot_all.astype(BF16), before) + run_ref[:, 0:1]
    for kk in range(TOP_K):
        idx_ref[kk:kk + 1, :] = idxs[kk]
        rank_ref[kk:kk + 1, :] = jnp.sum(jnp.where(hots[kk], seen, 0.0), axis=0, keepdims=True).astype(I32)
        gate_ref[kk:kk + 1, :] = exps[kk] / denom
    run_ref[...] = run_ref[...] + jnp.sum(hot_all, axis=1, keepdims=True)
    cnt_ref[...] = run_ref[...].astype(I32)


def _router(x1, sc, sh, router_wt, router_b):
    n, d = x1.shape
    tile = PROJ_TILE
    lane_spec = pl.BlockSpec((TOP_K, tile), lambda i: (0, i))
    return pl.pallas_call(
        _router_kernel,
        grid=(n // tile,),
        in_specs=[_row_spec(tile, d), _const_spec((1, d)), _const_spec((1, d)),
                  _const_spec((N_EXPERTS, d)), _const_spec((N_EXPERTS, 128))],
        out_specs=[lane_spec, lane_spec, lane_spec, _const_spec((N_EXPERTS, 128))],
        out_shape=[jax.ShapeDtypeStruct((TOP_K, n), I32), jax.ShapeDtypeStruct((TOP_K, n), I32),
                   jax.ShapeDtypeStruct((TOP_K, n), F32), jax.ShapeDtypeStruct((N_EXPERTS, 128), I32)],
        scratch_shapes=[pltpu.VMEM((N_EXPERTS, 128), F32)],
        compiler_params=_params(),
        name="router",
    )(x1, sc, sh, router_wt, router_b)


ROW_SUB = D_MODEL // 128


def _row_copy(src, src_row, dst, dst_row, sem):
    s0 = pl.multiple_of(src_row * ROW_SUB, ROW_SUB)
    d0 = pl.multiple_of(dst_row * ROW_SUB, ROW_SUB)
    return pltpu.make_async_copy(src.at[pl.ds(s0, ROW_SUB), :], dst.at[pl.ds(d0, ROW_SUB), :], sem)


def _to_row_tiles(ref, x):
    rows = x.shape[0]
    for s in range(ROW_SUB):
        ref[pl.ds(s, rows, stride=ROW_SUB), :] = x[:, s * 128:(s + 1) * 128]


def _from_row_tiles(ref, rows):
    return jnp.concatenate([ref[pl.ds(s, rows, stride=ROW_SUB), :] for s in range(ROW_SUB)], axis=1)


def _dispatch_kernel(dest_ref, pend_ref, x_ref, sc_ref, sh_ref, xs_ref, h_ref, zero_ref, sem):
    i = pl.program_id(0)
    tile = x_ref.shape[0]

    def zero_copy(e):
        r0 = pl.multiple_of((pend_ref[e] - MOE_ROWS) * ROW_SUB, MOE_ROWS * ROW_SUB)
        return pltpu.make_async_copy(zero_ref, xs_ref.at[pl.ds(r0, MOE_ROWS * ROW_SUB), :], sem)

    def has_rows(e):
        return pend_ref[e] > jnp.where(e > 0, pend_ref[jnp.maximum(e - 1, 0)], 0)

    @pl.when(i == 0)
    def _():
        zero_ref[...] = jnp.zeros_like(zero_ref)

        def start(e, c):
            @pl.when(has_rows(e))
            def _():
                zero_copy(e).start()
            return c

        def wait(e, c):
            @pl.when(has_rows(e))
            def _():
                zero_copy(e).wait()
            return c

        lax.fori_loop(0, N_EXPERTS, start, 0)
        lax.fori_loop(0, N_EXPERTS, wait, 0)

    _to_row_tiles(h_ref, x_ref[...] * (1.0 + sc_ref[...]) + sh_ref[...])
    base = i * (tile * TOP_K)

    def start_rows(t, c):
        for kk in range(TOP_K):
            _row_copy(h_ref, t, xs_ref, dest_ref[base + t * TOP_K + kk], sem).start()
        return c

    def wait_rows(t, c):
        for kk in range(TOP_K):
            _row_copy(h_ref, t, xs_ref, dest_ref[base + t * TOP_K + kk], sem).wait()
        return c

    lax.fori_loop(0, tile, start_rows, 0)
    lax.fori_loop(0, tile, wait_rows, 0)


def _dispatch(dest_flat, pend, x1, sc, sh, cap):
    n, d = x1.shape
    tile = MOE_TILE
    grid_spec = pltpu.PrefetchScalarGridSpec(
        num_scalar_prefetch=2,
        grid=(n // tile,),
        in_specs=[pl.BlockSpec((tile, d), lambda i, *_: (i, 0)),
                  pl.BlockSpec((1, d), lambda i, *_: (0, 0)),
                  pl.BlockSpec((1, d), lambda i, *_: (0, 0))],
        out_specs=pl.BlockSpec(memory_space=pl.ANY),
        scratch_shapes=[pltpu.VMEM((tile * ROW_SUB, 128), F32), pltpu.VMEM((MOE_ROWS * ROW_SUB, 128), F32),
                        pltpu.SemaphoreType.DMA],
    )
    return pl.pallas_call(
        _dispatch_kernel,
        grid_spec=grid_spec,
        out_shape=jax.ShapeDtypeStruct((cap * ROW_SUB, 128), F32),
        compiler_params=_params(),
        name="moe_dispatch",
    )(dest_flat, pend, x1, sc, sh)


def _expert_kernel(be_ref, nb_ref, xs_ref, wu_ref, bu_ref, wd_ref, bd_ref, ys_ref, wu_s, wd_s):
    b = pl.program_id(0)

    @pl.when(b < nb_ref[0])
    def _():
        prev = be_ref[jnp.maximum(b - 1, 0)]

        @pl.when((b == 0) | (be_ref[b] != prev))
        def _():
            wu_s[...] = wu_ref[0].astype(BF16)
            wd_s[...] = wd_ref[0].astype(BF16)

        up = _dot(_from_row_tiles(xs_ref, MOE_ROWS).astype(BF16), wu_s[...]) + bu_ref[0]
        glu = jnp.minimum(up[:, 0:D_FF], SWIGLU_LIMIT)
        lin = jnp.clip(up[:, D_FF:2 * D_FF], -SWIGLU_LIMIT, SWIGLU_LIMIT)
        act = glu * _sigmoid(SWIGLU_ALPHA * glu) * (lin + 1.0)
        _to_row_tiles(ys_ref, _dot(act.astype(BF16), wd_s[...]) + bd_ref[0])


def _experts(block_e, n_used, xs, w_up, b_up, w_down, b_down):
    d = D_MODEL
    n_blocks = xs.shape[0] // (MOE_ROWS * ROW_SUB)

    def blk(b, be, nb):
        return jnp.minimum(b, nb[0] - 1)

    grid_spec = pltpu.PrefetchScalarGridSpec(
        num_scalar_prefetch=2,
        grid=(n_blocks,),
        in_specs=[pl.BlockSpec((MOE_ROWS * ROW_SUB, 128), lambda b, be, nb: (blk(b, be, nb), 0)),
                  pl.BlockSpec((1, d, 2 * D_FF), lambda b, be, nb: (be[blk(b, be, nb)], 0, 0)),
                  pl.BlockSpec((1, 1, 2 * D_FF), lambda b, be, nb: (be[blk(b, be, nb)], 0, 0)),
                  pl.BlockSpec((1, D_FF, d), lambda b, be, nb: (be[blk(b, be, nb)], 0, 0)),
                  pl.BlockSpec((1, 1, d), lambda b, be, nb: (be[blk(b, be, nb)], 0, 0))],
        out_specs=pl.BlockSpec((MOE_ROWS * ROW_SUB, 128), lambda b, be, nb: (blk(b, be, nb), 0)),
        scratch_shapes=[pltpu.VMEM((d, 2 * D_FF), BF16), pltpu.VMEM((D_FF, d), BF16)],
    )
    return pl.pallas_call(
        _expert_kernel,
        grid_spec=grid_spec,
        out_shape=jax.ShapeDtypeStruct(xs.shape, F32),
        compiler_params=_params(),
        name="moe_experts",
    )(block_e, n_used, xs, w_up, b_up.reshape(N_EXPERTS, 1, 2 * D_FF), w_down, b_down.reshape(N_EXPERTS, 1, d))


def _combine_kernel(alpha, dest_ref, ys_ref, gates_ref, x_ref, gt_ref, g_ref, b_ref, o_ref, buf_ref, sem):
    i = pl.program_id(0)
    tile = x_ref.shape[0]
    base = i * (tile * TOP_K)

    def start_rows(t, c):
        for kk in range(TOP_K):
            _row_copy(ys_ref, dest_ref[base + t * TOP_K + kk], buf_ref.at[kk], t, sem).start()
        return c

    def wait_rows(t, c):
        for kk in range(TOP_K):
            _row_copy(ys_ref, dest_ref[base + t * TOP_K + kk], buf_ref.at[kk], t, sem).wait()
        return c

    lax.fori_loop(0, tile, start_rows, 0)
    lax.fori_loop(0, tile, wait_rows, 0)
    gates = gates_ref[...]
    ffn = _from_row_tiles(buf_ref.at[0], tile) * gates[:, 0:1]
    for kk in range(1, TOP_K):
        ffn = ffn + _from_row_tiles(buf_ref.at[kk], tile) * gates[:, kk:kk + 1]
    o_ref[...] = _layer_norm_rows(alpha * x_ref[...] + (1.0 + gt_ref[...]) * ffn, g_ref[...], b_ref[...])


def _combine(alpha, dest_flat, ys, gates_t, x1, gt, ln_g, ln_b):
    n, d = x1.shape
    tile = MOE_TILE
    vec = pl.BlockSpec((1, d), lambda i, *_: (0, 0))
    grid_spec = pltpu.PrefetchScalarGridSpec(
        num_scalar_prefetch=1,
        grid=(n // tile,),
        in_specs=[pl.BlockSpec(memory_space=pl.ANY),
                  pl.BlockSpec((tile, TOP_K), lambda i, *_: (i, 0)),
                  pl.BlockSpec((tile, d), lambda i, *_: (i, 0)), vec, vec, vec],
        out_specs=pl.BlockSpec((tile, d), lambda i, *_: (i, 0)),
        scratch_shapes=[pltpu.VMEM((TOP_K, tile * ROW_SUB, 128), F32), pltpu.SemaphoreType.DMA],
    )
    return pl.pallas_call(
        functools.partial(_combine_kernel, alpha),
        grid_spec=grid_spec,
        out_shape=jax.ShapeDtypeStruct((n, d), F32),
        compiler_params=_params(),
        name="moe_combine_ln",
    )(dest_flat, ys, gates_t, x1, gt, ln_g, ln_b)


def _pad_rows(w, rows, offset=0):
    out = jnp.zeros((rows, w.shape[1]), w.dtype)
    return out.at[offset:offset + w.shape[0]].set(w)


def _moe(alpha, x1, sc, sh, gt, router_w, router_b, w_up, b_up, w_down, b_down, ln_g, ln_b):
    n, d = x1.shape
    router_bias = jnp.broadcast_to(router_b[:, None], (N_EXPERTS, 128))
    idx, rank, gates, counts = _router(x1, sc, sh, router_w.T, router_bias)
    counts = counts[:, 0]
    padded = (counts + MOE_ROWS - 1) // MOE_ROWS * MOE_ROWS
    pend = jnp.cumsum(padded).astype(I32)
    pstart = pend - padded
    dest = (pstart[idx] + rank).T.reshape(n * TOP_K)
    n_blocks = -(-(n * TOP_K + N_EXPERTS * (MOE_ROWS - 1)) // MOE_ROWS)
    block_e = jnp.minimum(jnp.searchsorted(pend, jnp.arange(n_blocks, dtype=I32) * MOE_ROWS, side='right'),
                          N_EXPERTS - 1).astype(I32)
    n_used = (pend[-1:] // MOE_ROWS).astype(I32)
    xs = _dispatch(dest, pend, x1, sc, sh, n_blocks * MOE_ROWS)
    ys = _experts(block_e, n_used, xs, w_up, b_up, w_down, b_down)
    return _combine(alpha, dest, ys, gates.T, x1, gt, ln_g, ln_b)


def kernel(x, c, positions, ada_w, ada_b, w_in, w_in_vres, tshift_mu, tshift_mu_vres, rwkv_w0, rwkv_w2, rwkv_a0, rwkv_a2, rwkv_g2, rwkv_kk, rwkv_ka, rwkv_rk, rwkv_ln_g, rwkv_ln_b, rwkv_v0, rwkv_v2, gdn_conv_w, gdn_a_log, gdn_dt_bias, gdn_norm_g, ret_norm_g, ret_norm_b, w_out, ln1_g, ln1_b, router_w, router_b, exp_w_up, exp_b_up, exp_w_down, exp_b_down, ln2_g, ln2_b):
    bsz, seq, d = x.shape
    assert bsz == 1 and d == D_MODEL and seq % MIX_TILE == 0
    depth = ada_w.shape[0]
    alpha = (2 * depth) ** 0.25
    n = seq
    x2 = x.reshape(n, d)
    ada = _ada(c, ada_w, ada_b)

    half = RET_HD // 2
    inv_freq = ROPE_BASE ** (-jnp.arange(half, dtype=F32) / half)
    freq_full = jnp.tile(inv_freq, 2 * RET_HEADS)[None, :]
    log_gamma = jnp.log1p(-jnp.exp2(-5.0 - jnp.arange(RET_HEADS, dtype=F32)))
    lg_full = jnp.repeat(log_gamma, RET_HD)[None, :]
    lg_stack = jnp.repeat(log_gamma, RET_CHUNK)[None, :]
    lane = np.arange(RET_W)
    swap_np = np.zeros((RET_W, RET_W), np.float32)
    swap_np[np.where(lane % RET_HD < half, lane + half, lane - half), lane] = 1.0
    swap = jnp.asarray(swap_np, BF16)
    sel_np = np.zeros((2, 128, GDN_W), np.float32)
    for hh in range(GDN_HEADS):
        sel_np[0, X_GA + hh, hh * GDN_HD:(hh + 1) * GDN_HD] = 1.0
        sel_np[1, X_GB + hh, hh * GDN_HD:(hh + 1) * GDN_HD] = 1.0
    sel_a, sel_b = jnp.asarray(sel_np[0], BF16), jnp.asarray(sel_np[1], BF16)
    pos_col = positions.reshape(n, 1)

    v_first = None
    for l in range(depth):
        sh_mix, sc_mix, gt_mix, sh_ffn, sc_ffn, gt_ffn = (ada[l, :, i * d:(i + 1) * d] for i in range(6))
        w_l = w_in[l]
        zeros32 = jnp.zeros((d, 32), F32)
        vres_w = zeros32 if l == 0 else w_in_vres[l - 1]
        x_cols = jnp.concatenate([vres_w, w_l[:, 3072:3080], jnp.zeros((d, 88), F32)], axis=1)
        w_cat = jnp.concatenate([w_l[:, 0:1024], w_l[:, 1024:3072], w_l[:, 3080:4104], x_cols], axis=1).astype(BF16)
        mu_a = tshift_mu[l][None, :]
        mu_v = jnp.zeros((32,), F32) if l == 0 else tshift_mu_vres[l - 1]
        mu_x = jnp.concatenate([mu_v, jnp.zeros((96,), F32)])[None, :]
        pa, pb, pc, px = _inproj(x2, sc_mix, sh_mix, w_cat, mu_a, mu_x)

        v0 = jnp.zeros((RWKV_W,), F32) if l == 0 else rwkv_v0[l - 1]
        vec_a = jnp.stack([rwkv_w0[l], rwkv_a0[l], rwkv_kk[l], rwkv_ka[l], rwkv_rk[l].reshape(RWKV_W),
                           rwkv_ln_g[l], rwkv_ln_b[l], v0])
        w2p = _pad_rows(rwkv_w2[l], 128, 0).astype(BF16)
        a2p = _pad_rows(rwkv_a2[l], 128, 64).astype(BF16)
        g2 = rwkv_g2[l].astype(BF16)
        if l == 0:
            y_a, v_first = _rwkv(pa, None, None, vec_a, w2p, a2p, g2, None)
        else:
            v2p = _pad_rows(rwkv_v2[l - 1], 128, X_VRES).astype(BF16)
            y_a = _rwkv(pa, px, v_first, vec_a, w2p, a2p, g2, v2p)

        vec_b = jnp.concatenate([jnp.stack([jnp.repeat(gdn_a_log[l], GDN_HD), jnp.repeat(gdn_dt_bias[l], GDN_HD),
                                            jnp.tile(gdn_norm_g[l], GDN_HEADS)]), jnp.zeros((5, GDN_W), F32)])
        y_b = _gdn(pb, px, gdn_conv_w[l], vec_b, sel_a, sel_b)

        vec_c = jnp.concatenate([jnp.stack([ret_norm_g[l], ret_norm_b[l]]), jnp.zeros((6, RET_W), F32)])
        y_c = _ret(pc, pos_col, freq_full, lg_full, lg_stack, swap, vec_c)

        x1 = _outproj(alpha, y_a, y_b, y_c, x2, w_out[l].astype(BF16), gt_mix, ln1_g[l][None, :], ln1_b[l][None, :])
        x2 = _moe(alpha, x1, sc_ffn, sh_ffn, gt_ffn, router_w[l], router_b[l], exp_w_up[l], exp_b_up[l],
                  exp_w_down[l], exp_b_down[l], ln2_g[l][None, :], ln2_b[l][None, :])
    return x2.reshape(bsz, seq, d)
```

```python
import functools

import jax
import jax.numpy as jnp
from jax import lax
from jax.experimental import pallas as pl
from jax.experimental.pallas import tpu as pltpu

F32 = jnp.float32
BF16 = jnp.bfloat16
I32 = jnp.int32

D_MODEL = 1024
RWKV_HEADS, RWKV_HD, RWKV_W = 4, 64, 256
RWKV_GN_EPS = 64e-5
GDN_HEADS, GDN_HD, GDN_W = 4, 128, 512
GDN_CONV = 4
RET_HEADS, RET_HD, RET_W = 4, 64, 256
RET_CHUNK = 128
ROPE_BASE = 10000.0
N_EXPERTS, TOP_K, D_FF = 32, 4, 1024
SWIGLU_LIMIT, SWIGLU_ALPHA = 7.0, 1.702
LN_EPS, RMS_EPS = 1e-5, 1e-6

CHUNK = 64
STACK = 4 * CHUNK
MIX_TILE = 512
PROJ_TILE = 256
MOE_ROWS = 256
MOE_TILE = 256
NEG_BIG = -1e30
VMEM_LIMIT = 56 * 1024 * 1024

COL_A, COL_B, COL_C, COL_X = 0, 1024, 3072, 4096
N_PROJ = 4224
X_VRES, X_GA, X_GB = 0, 32, 36


def _dot(a, b):
    return jnp.dot(a, b, preferred_element_type=F32)


def _mm(a, b):
    return _dot(a.astype(BF16), b.astype(BF16))


def _mm_nt(a, b):
    return lax.dot_general(a.astype(BF16), b.astype(BF16), (((1,), (1,)), ((), ())),
                           preferred_element_type=F32)


def _split3(x):
    x1 = x.astype(BF16)
    r1 = x - x1.astype(F32)
    x2 = r1.astype(BF16)
    x3 = (r1 - x2.astype(F32)).astype(BF16)
    return x1, x2, x3


def _sel_mm(sel, x):
    x1, x2, x3 = _split3(x)
    return _dot(sel, x1) + (_dot(sel, x2) + _dot(sel, x3))


def _sel_mm_nt(sel, x):
    dn = (((1,), (1,)), ((), ()))
    x1, x2, x3 = _split3(x)
    d = lambda a, b: lax.dot_general(a, b, dn, preferred_element_type=F32)
    return d(sel, x1) + (d(sel, x2) + d(sel, x3))


def _mm3(a, b):
    a1 = a.astype(BF16)
    a2 = (a - a1.astype(F32)).astype(BF16)
    b1 = b.astype(BF16)
    b2 = (b - b1.astype(F32)).astype(BF16)
    return _dot(a1, b1) + (_dot(a1, b2) + _dot(a2, b1))


def _mm3_nt(a, b):
    dn = (((1,), (1,)), ((), ()))
    d = lambda x, y: lax.dot_general(x, y, dn, preferred_element_type=F32)
    a1 = a.astype(BF16)
    a2 = (a - a1.astype(F32)).astype(BF16)
    b1 = b.astype(BF16)
    b2 = (b - b1.astype(F32)).astype(BF16)
    return d(a1, b1) + (d(a1, b2) + d(a2, b1))


def _iota2(shape, axis):
    return lax.broadcasted_iota(I32, shape, axis)


def _softplus(x):
    return jnp.maximum(x, 0.0) + jnp.log1p(jnp.exp(-jnp.abs(x)))


def _sigmoid(x):
    return jax.nn.sigmoid(x)


def _silu(x):
    return x * jax.nn.sigmoid(x)


def _same_block(n_rows, n_cols, block):
    r = _iota2((n_rows, n_cols), 0)
    c = _iota2((n_rows, n_cols), 1)
    return (r // block) == (c // block), r, c


def _block_cumsum_mat(n, block):
    same, r, c = _same_block(n, n, block)
    return jnp.where(same & (c <= r), 1.0, 0.0).astype(BF16)


def _head_masks(width, head_dim):
    lane = _iota2((1, width), 1)
    return [jnp.where((lane // head_dim) == h, 1.0, 0.0).astype(F32) for h in range(width // head_dim)]


def _stack_heads(x, masks):
    return jnp.concatenate([x * m for m in masks], axis=0)


def _fold_heads(x4, n_heads):
    c = x4.shape[0] // n_heads
    out = x4[0:c]
    for h in range(1, n_heads):
        out = out + x4[h * c:(h + 1) * c]
    return out


def _chunk_last_row(x, chunk):
    rows, width = x.shape
    x3 = x.reshape(rows // chunk, chunk, width)
    return jnp.broadcast_to(x3[:, chunk - 1:chunk, :], x3.shape).reshape(rows, width)


def _unit_lower_inverse(m):
    n = m.shape[0]
    eye = jnp.where(_iota2((n, n), 0) == _iota2((n, n), 1), 1.0, 0.0).astype(F32)
    t = eye - m
    p = m
    for _ in range(5):
        p = _mm(p, p)
        t = t + _mm(t, p)
    return t


def _layer_norm_rows(x, g, b):
    mu = jnp.mean(x, axis=-1, keepdims=True)
    xc = x - mu
    var = jnp.mean(xc * xc, axis=-1, keepdims=True)
    return xc * lax.rsqrt(var + LN_EPS) * g + b


def _shift_down(x, prev8, d):
    rolled = pltpu.roll(x, d, axis=0)
    head = pltpu.roll(jnp.concatenate([prev8, x[0:8]], axis=0), d, axis=0)[8:16]
    return jnp.concatenate([head, rolled[8:]], axis=0)


def _head_sums(x, head_dim):
    lane = _iota2((1, 128), 1)
    parts = []
    for j in range(x.shape[1] // 128):
        slab = x[:, j * 128:(j + 1) * 128]
        if head_dim == 128:
            parts.append(jnp.broadcast_to(jnp.sum(slab, axis=-1, keepdims=True), slab.shape))
        else:
            lo = lane < 64
            s_lo = jnp.sum(jnp.where(lo, slab, 0.0), axis=-1, keepdims=True)
            s_hi = jnp.sum(jnp.where(lo, 0.0, slab), axis=-1, keepdims=True)
            parts.append(jnp.where(lo, s_lo, s_hi))
    return jnp.concatenate(parts, axis=1)


def _params(n_grid_axes=1, vmem=VMEM_LIMIT):
    return pltpu.CompilerParams(dimension_semantics=("arbitrary",) * n_grid_axes, vmem_limit_bytes=vmem)


def _row_spec(tile, width):
    return pl.BlockSpec((tile, width), lambda i: (i, 0))


def _const_spec(shape):
    zeros = (0,) * len(shape)
    return pl.BlockSpec(shape, lambda i: zeros)


def _ada_kernel(c_ref, w_ref, b_ref, o_ref):
    c = c_ref[...]
    o_ref[0] = _mm3(_silu(c), w_ref[0]) + b_ref[0]


def _ada(c, ada_w, ada_b):
    depth, d, d6 = ada_w.shape
    blk = 1024
    c8 = jnp.broadcast_to(c, (8, d))
    out = pl.pallas_call(
        _ada_kernel,
        grid=(depth, d6 // blk),
        in_specs=[pl.BlockSpec((8, d), lambda l, j: (0, 0)),
                  pl.BlockSpec((1, d, blk), lambda l, j: (l, 0, j)),
                  pl.BlockSpec((1, 1, blk), lambda l, j: (l, 0, j))],
        out_specs=pl.BlockSpec((1, 8, blk), lambda l, j: (l, 0, j)),
        out_shape=jax.ShapeDtypeStruct((depth, 8, d6), F32),
        compiler_params=_params(2),
        name="ada",
    )(c8, ada_w, ada_b.reshape(depth, 1, d6))
    return out[:, 0:1, :]


def _inproj_kernel(x_ref, sc_ref, sh_ref, w_ref, mua_ref, mux_ref, pa_ref, pb_ref, pc_ref, px_ref, carry_ref):
    @pl.when(pl.program_id(0) == 0)
    def _():
        carry_ref[...] = jnp.zeros_like(carry_ref)

    tile = x_ref.shape[0]
    h = (x_ref[...] * (1.0 + sc_ref[...]) + sh_ref[...]).astype(BF16)
    pa = _dot(h, w_ref[:, COL_A:COL_B])
    px = _dot(h, w_ref[:, COL_X:N_PROJ])
    pb_ref[...] = _dot(h, w_ref[:, COL_B:COL_C])
    pc_ref[...] = _dot(h, w_ref[:, COL_C:COL_X])
    prev_a = _shift_down(pa, carry_ref[:, 0:1024], 1)
    prev_x = _shift_down(px, carry_ref[:, 1024:1152], 1)
    carry_ref[:, 0:1024] = pa[tile - 8:tile, :]
    carry_ref[:, 1024:1152] = px[tile - 8:tile, :]
    pa_ref[...] = pa + (prev_a - pa) * mua_ref[...]
    px_ref[...] = px + (prev_x - px) * mux_ref[...]


def _inproj(x2, sc, sh, w_cat, mu_a, mu_x):
    n, d = x2.shape
    tile = PROJ_TILE
    return pl.pallas_call(
        _inproj_kernel,
        grid=(n // tile,),
        in_specs=[_row_spec(tile, d), _const_spec((1, d)), _const_spec((1, d)),
                  _const_spec((d, N_PROJ)), _const_spec((1, 1024)), _const_spec((1, 128))],
        out_specs=[_row_spec(tile, 1024), _row_spec(tile, 2048), _row_spec(tile, 1024), _row_spec(tile, 128)],
        out_shape=[jax.ShapeDtypeStruct((n, 1024), F32), jax.ShapeDtypeStruct((n, 2048), F32),
                   jax.ShapeDtypeStruct((n, 1024), F32), jax.ShapeDtypeStruct((n, 128), F32)],
        scratch_shapes=[pltpu.VMEM((8, 1152), F32)],
        compiler_params=_params(),
        name="inproj",
    )(x2, sc, sh, w_cat, mu_a, mu_x)


def _rwkv_kernel(has_vres, *refs):
    if has_vres:
        (pa_ref, px_ref, vf_ref, vec_ref, w2_ref, a2_ref, g2_ref, v2_ref,
         y_ref, state_ref, rt_s, at_s, kt_s, bt_s, kh_s, bh_s, v_s, gc_s, y_s) = refs
    else:
        (pa_ref, vec_ref, w2_ref, a2_ref, g2_ref,
         y_ref, vout_ref, state_ref, rt_s, at_s, kt_s, bt_s, kh_s, bh_s, v_s, gc_s, y_s) = refs

    @pl.when(pl.program_id(0) == 0)
    def _():
        state_ref[...] = jnp.zeros_like(state_ref)

    tile = pa_ref.shape[0]
    w0, a0, k_k, k_a, r_k, ln_g, ln_b, v0 = (vec_ref[i:i + 1, :] for i in range(8))
    r = pa_ref[:, 0:256]
    k = pa_ref[:, 256:512]
    v = pa_ref[:, 512:768]
    wa = pa_ref[:, 768:896]
    g_lo = pa_ref[:, 896:1024]

    w_raw = -_softplus(-(w0 + _mm(jnp.tanh(wa), w2_ref[...]))) - 0.5
    log_decay = -jnp.exp(w_raw)
    a = _sigmoid(a0 + _mm(wa, a2_ref[...]))
    gate = _mm(_sigmoid(g_lo), g2_ref[...])
    if has_vres:
        v = v + (vf_ref[...] - v) * _sigmoid(v0 + _mm(px_ref[...], v2_ref[...]))
    else:
        vout_ref[...] = v

    kk = k * k_k
    kk = kk * lax.rsqrt(_head_sums(kk * kk, RWKV_HD) + 1e-12)
    k = k * (1.0 + (a - 1.0) * k_a)
    bonus = _head_sums(r * k * r_k, RWKV_HD) * v

    cl = _sel_mm(_block_cumsum_mat(tile, CHUNK), log_decay)
    cl_end = _chunk_last_row(cl, CHUNK)
    e_inv = jnp.exp(-cl)
    e_end = jnp.exp(cl_end - cl)
    kka = kk * a
    rt_s[...] = r * jnp.exp(cl)
    at_s[...] = -kk * jnp.exp(cl - log_decay)
    kt_s[...] = k * e_inv
    bt_s[...] = kka * e_inv
    kh_s[...] = k * e_end
    bh_s[...] = kka * e_end
    v_s[...] = v
    gc_s[...] = jnp.exp(cl_end)

    masks = _head_masks(RWKV_W, RWKV_HD)
    same, row, col = _same_block(STACK, STACK, CHUNK)
    strict = same & (row > col)
    incl = same & (row >= col)
    same_head = (_iota2((RWKV_W, RWKV_W), 0) // RWKV_HD) == (_iota2((RWKV_W, RWKV_W), 1) // RWKV_HD)

    def intra(c):
        rows = pl.ds(pl.multiple_of(c * CHUNK, CHUNK), CHUNK)
        rt_c, v_c = rt_s[rows, :], v_s[rows, :]
        at4 = _stack_heads(at_s[rows, :], masks)
        ar4 = jnp.concatenate([at4, _stack_heads(rt_c, masks)], axis=0)
        kb4 = jnp.concatenate([_stack_heads(kt_s[rows, :], masks), _stack_heads(bt_s[rows, :], masks)], axis=0)
        v4 = _stack_heads(v_c, masks)
        s = _mm_nt(ar4, kb4)
        a_ak = jnp.where(strict, s[0:STACK, 0:STACK], 0.0)
        a_ab = jnp.where(strict, s[0:STACK, STACK:2 * STACK], 0.0)
        a_rk = jnp.where(incl, s[STACK:2 * STACK, 0:STACK], 0.0)
        a_rb = jnp.where(incl, s[STACK:2 * STACK, STACK:2 * STACK], 0.0)
        t_inv = _unit_lower_inverse(-a_ab)
        return dict(rows=rows, c=c, rt=rt_c, v=v_c, a_rb=a_rb,
                    w4=_mm(t_inv, at4), u0=_mm(t_inv, _mm(a_ak, v4)), yv=_mm(a_rk, v4),
                    kb=jnp.concatenate([kh_s[rows, :], bh_s[rows, :]], axis=0))

    def advance(p):
        state = state_ref[...]
        u4 = _mm_nt(p["w4"], state) + p["u0"]
        y4 = p["yv"] + _mm(p["a_rb"], u4)
        y_s[p["rows"], :] = _mm_nt(p["rt"], state) + _fold_heads(y4, RWKV_HEADS)
        zt = jnp.concatenate([p["v"], _fold_heads(u4, RWKV_HEADS)], axis=0).T
        decay = gc_s[pl.ds(p["c"] * CHUNK, 1), :]
        state_ref[...] = state * decay + jnp.where(same_head, _mm(zt, p["kb"]), 0.0)

    def pair_body(i, carry):
        first, second = intra(2 * i), intra(2 * i + 1)
        advance(first)
        advance(second)
        return carry

    lax.fori_loop(0, tile // (2 * CHUNK), pair_body, 0)

    y = y_s[...]
    inv_n = 1.0 / RWKV_HD
    mu = _head_sums(y, RWKV_HD) * inv_n
    yc = y - mu
    var = _head_sums(yc * yc, RWKV_HD) * inv_n
    yn = yc * lax.rsqrt(var + RWKV_GN_EPS) * ln_g + ln_b
    y_ref[...] = (yn + bonus) * gate


def _rwkv(pa, px, v_first, vec, w2p, a2p, g2, v2p):
    n = pa.shape[0]
    tile = MIX_TILE
    has_vres = v_first is not None
    big = [pltpu.VMEM((tile, RWKV_W), F32) for _ in range(9)]
    scratch = [pltpu.VMEM((RWKV_W, RWKV_W), F32)] + big
    wspec = _const_spec((128, RWKV_W))
    if has_vres:
        in_specs = [_row_spec(tile, 1024), _row_spec(tile, 128), _row_spec(tile, RWKV_W),
                    _const_spec((8, RWKV_W)), wspec, wspec, wspec, wspec]
        args = (pa, px, v_first, vec, w2p, a2p, g2, v2p)
        out_specs = _row_spec(tile, RWKV_W)
        out_shape = jax.ShapeDtypeStruct((n, RWKV_W), F32)
    else:
        in_specs = [_row_spec(tile, 1024), _const_spec((8, RWKV_W)), wspec, wspec, wspec]
        args = (pa, vec, w2p, a2p, g2)
        out_specs = [_row_spec(tile, RWKV_W), _row_spec(tile, RWKV_W)]
        out_shape = [jax.ShapeDtypeStruct((n, RWKV_W), F32), jax.ShapeDtypeStruct((n, RWKV_W), F32)]
    return pl.pallas_call(
        functools.partial(_rwkv_kernel, has_vres),
        grid=(n // tile,),
        in_specs=in_specs, out_specs=out_specs, out_shape=out_shape,
        scratch_shapes=scratch,
        compiler_params=_params(),
        name="rwkv7_vres" if has_vres else "rwkv7",
    )(*args)


def _gdn_kernel(pb_ref, px_ref, convw_ref, vec_ref, vecx_ref, y_ref,
                state_ref, carry_ref, q_s, qd_s, k_s, kb_s, kd_s, rhs_s, gcs_s, gl_s, o_s):
    @pl.when(pl.program_id(0) == 0)
    def _():
        state_ref[...] = jnp.zeros_like(state_ref)
        carry_ref[...] = jnp.zeros_like(carry_ref)

    tile = pb_ref.shape[0]
    width3 = 3 * GDN_W
    norm_g = vec_ref[0:1, :]

    x = pb_ref[:, 0:width3]
    prev8 = carry_ref[...]
    acc = x * convw_ref[GDN_CONV - 1:GDN_CONV, :]
    for d in range(1, GDN_CONV):
        acc = acc + _shift_down(x, prev8, d) * convw_ref[GDN_CONV - 1 - d:GDN_CONV - d, :]
    carry_ref[...] = x[tile - 8:tile, :]
    qkv = _silu(acc)
    q = qkv[:, 0:GDN_W]
    k = qkv[:, GDN_W:2 * GDN_W]
    v = qkv[:, 2 * GDN_W:width3]
    q = q * lax.rsqrt(_head_sums(q * q, GDN_HD) + 1e-12) * (GDN_HD ** -0.5)
    k = k * lax.rsqrt(_head_sums(k * k, GDN_HD) + 1e-12)

    px = px_ref[...]
    g_x = -jnp.exp(vecx_ref[0:1, :]) * _softplus(px + vecx_ref[1:2, :])
    beta_x = _sigmoid(px)
    gc_x = _sel_mm(_block_cumsum_mat(tile, CHUNK), g_x)
    gl_x = _chunk_last_row(gc_x, CHUNK)
    eg_x = jnp.exp(gc_x)
    ed_x = jnp.exp(gl_x - gc_x)
    el_x = jnp.exp(gl_x)
    for h in range(GDN_HEADS):
        hs = slice(h * GDN_HD, (h + 1) * GDN_HD)
        col = lambda t, lane0: t[:, lane0 + h:lane0 + h + 1]
        beta, eg = col(beta_x, X_GB), col(eg_x, X_GA)
        kb = k[:, hs] * beta
        q_s[:, hs] = q[:, hs]
        qd_s[:, hs] = q[:, hs] * eg
        k_s[:, hs] = k[:, hs]
        kb_s[:, hs] = kb
        kd_s[:, hs] = k[:, hs] * col(ed_x, X_GA)
        rhs_s[:, hs] = v[:, hs] * beta
        rhs_s[:, GDN_W + h * GDN_HD:GDN_W + (h + 1) * GDN_HD] = kb * eg
        gcs_s[:, hs] = jnp.broadcast_to(col(gc_x, X_GA), (tile, GDN_HD))
        gl_s[:, hs] = jnp.broadcast_to(col(el_x, X_GA), (tile, GDN_HD))

    same, rr, cc = _same_block(STACK, STACK, CHUNK)
    strict = same & (rr > cc)
    incl = same & (rr >= cc)
    lane = _iota2((STACK, GDN_HD), 1)
    pick3 = jnp.where(lane < 3, 1.0, 0.0).astype(BF16)
    row_head = [jnp.where((_iota2((1, STACK), 1) // CHUNK) == h, 1.0, 0.0).astype(F32) for h in range(GDN_HEADS)]

    def stack(x):
        return jnp.concatenate([x[:, h * GDN_HD:(h + 1) * GDN_HD] for h in range(GDN_HEADS)], axis=0)

    def unstack(y):
        return jnp.concatenate([y[h * CHUNK:(h + 1) * CHUNK] for h in range(GDN_HEADS)], axis=1)

    def per_head(fn):
        return [fn(h, slice(h * CHUNK, (h + 1) * CHUNK)) for h in range(GDN_HEADS)]

    def intra(c):
        rows = pl.ds(pl.multiple_of(c * CHUNK, CHUNK), CHUNK)
        q_c, k_c, kb_c = stack(q_s[rows, :]), stack(k_s[rows, :]), stack(kb_s[rows, :])
        rhs = rhs_s[rows, :]
        rhs_c = jnp.concatenate([stack(rhs[:, 0:GDN_W]), stack(rhs[:, GDN_W:2 * GDN_W])], axis=1)
        gcol = stack(gcs_s[rows, :])
        col_form = jnp.concatenate([gcol, gcol], axis=1)
        g1, g2, g3 = _split3(gcol)
        zero = jnp.zeros_like(g1)
        pieces = jnp.where(lane == 0, g1, jnp.where(lane == 1, g2, jnp.where(lane == 2, g3, zero)))
        row_form = lax.dot_general(pick3, pieces, (((1,), (1,)), ((), ())), preferred_element_type=F32)
        decay = jnp.exp(jnp.where(incl, col_form - row_form, NEG_BIG))
        s = _mm_nt(jnp.concatenate([kb_c, q_c], axis=0), k_c)
        m_low = jnp.where(strict, s[0:STACK] * decay, 0.0)
        attn = jnp.where(incl, s[STACK:2 * STACK] * decay, 0.0)
        sol = _mm(_unit_lower_inverse(m_low), rhs_c)
        return dict(rows=rows, c=c, attn=attn, u=sol[:, 0:GDN_HD], w=sol[:, GDN_HD:2 * GDN_HD],
                    qd=stack(qd_s[rows, :]), kd_t=stack(kd_s[rows, :]).T)

    def advance(p):
        states = [state_ref[h] for h in range(GDN_HEADS)]
        v_new = p["u"] - jnp.concatenate(per_head(lambda h, r: _mm(p["w"][r], states[h])), axis=0)
        o_c = jnp.concatenate(per_head(lambda h, r: _mm(p["qd"][r], states[h])), axis=0) + _mm(p["attn"], v_new)
        o_s[p["rows"], :] = unstack(o_c)
        gl = gl_s[pl.ds(p["c"] * CHUNK, 1), :]
        for h in range(GDN_HEADS):
            state_ref[h] = (states[h] * gl[:, h * GDN_HD:(h + 1) * GDN_HD]
                            + _mm(p["kd_t"] * row_head[h], v_new))

    def pair_body(i, carry):
        first, second = intra(2 * i), intra(2 * i + 1)
        advance(first)
        advance(second)
        return carry

    lax.fori_loop(0, tile // (2 * CHUNK), pair_body, 0)

    o = o_s[...]
    ms = _head_sums(o * o, GDN_HD) * (1.0 / GDN_HD)
    z = pb_ref[:, width3:width3 + GDN_W]
    y_ref[...] = o * lax.rsqrt(ms + RMS_EPS) * norm_g * _silu(z)


def _gdn(pb, px, conv_w, vec, vec_x):
    n = pb.shape[0]
    tile = MIX_TILE
    wide = lambda w: pltpu.VMEM((tile, w), F32)
    scratch = [pltpu.VMEM((GDN_HEADS, GDN_HD, GDN_HD), F32), pltpu.VMEM((8, 3 * GDN_W), F32),
               wide(GDN_W), wide(GDN_W), wide(GDN_W), wide(GDN_W), wide(GDN_W), wide(2 * GDN_W),
               wide(GDN_W), wide(GDN_W), wide(GDN_W)]
    return pl.pallas_call(
        _gdn_kernel,
        grid=(n // tile,),
        in_specs=[_row_spec(tile, 2048), _row_spec(tile, 128), _const_spec((GDN_CONV, 3 * GDN_W)),
                  _const_spec((8, GDN_W)), _const_spec((8, 128))],
        out_specs=_row_spec(tile, GDN_W),
        out_shape=jax.ShapeDtypeStruct((n, GDN_W), F32),
        scratch_shapes=scratch,
        compiler_params=_params(),
        name="gdn",
    )(pb, px, conv_w, vec, vec_x)


def _ret_kernel(pc_ref, pos_ref, freq_ref, lg_ref, lgs_ref, vec_ref, y_ref,
                state_ref, dmat_ref, xi_ref, zeta_ref, q_s, k_s, v_s, o_s):
    n_stack = RET_HEADS * RET_CHUNK

    @pl.when(pl.program_id(0) == 0)
    def _():
        state_ref[...] = jnp.zeros_like(state_ref)
        same, r, c = _same_block(n_stack, n_stack, RET_CHUNK)
        dist = (r - c).astype(F32)
        dmat_ref[...] = jnp.exp(jnp.where(same & (r >= c), dist * lgs_ref[...], NEG_BIG))
        idx = (_iota2((n_stack, RET_W), 0) % RET_CHUNK).astype(F32)
        xi_ref[...] = jnp.exp((idx + 1.0) * lg_ref[...])
        zeta_ref[...] = jnp.exp((RET_CHUNK - 1.0 - idx) * lg_ref[...])

    tile = pc_ref.shape[0]
    ln_g, ln_b = vec_ref[0:1, :], vec_ref[1:2, :]
    ang = pos_ref[...].astype(F32) * freq_ref[...]
    cos = jnp.cos(ang)
    lane = _iota2((1, RET_W), 1)
    sin_signed = jnp.where((lane % RET_HD) < RET_HD // 2, -1.0, 1.0) * jnp.sin(ang)
    first_half = (_iota2((1, 128), 1) % RET_HD) < RET_HD // 2

    def partner(t):
        slabs = [t[:, j * 128:(j + 1) * 128] for j in range(RET_W // 128)]
        return jnp.concatenate([jnp.where(first_half, pltpu.roll(sl, 128 - RET_HD // 2, axis=1),
                                          pltpu.roll(sl, RET_HD // 2, axis=1)) for sl in slabs], axis=1)

    def rotary(t):
        return t * cos + partner(t) * sin_signed

    q_s[...] = rotary(pc_ref[:, 0:RET_W])
    k_s[...] = rotary(pc_ref[:, RET_W:2 * RET_W]) * (RET_HD ** -0.5)
    v_s[...] = pc_ref[:, 2 * RET_W:3 * RET_W]
    gate = pc_ref[:, 3 * RET_W:4 * RET_W]
    masks = _head_masks(RET_W, RET_HD)
    chunk_decay = jnp.exp(float(RET_CHUNK) * lg_ref[...])

    def chunk_body(c, carry):
        rows = pl.ds(pl.multiple_of(c * RET_CHUNK, RET_CHUNK), RET_CHUNK)
        q4 = _stack_heads(q_s[rows, :], masks)
        k4 = _stack_heads(k_s[rows, :], masks)
        v4 = _stack_heads(v_s[rows, :], masks)
        scores = _mm_nt(q4, k4) * dmat_ref[...]
        state = state_ref[...]
        o4 = _mm(scores, v4) + _mm(q4 * xi_ref[...], state)
        o_s[rows, :] = _fold_heads(o4, RET_HEADS)
        state_ref[...] = state * chunk_decay + _mm((k4 * zeta_ref[...]).T, v4)
        return carry

    lax.fori_loop(0, tile // RET_CHUNK, chunk_body, 0)

    o = o_s[...]
    inv_n = 1.0 / RET_HD
    mu = _head_sums(o, RET_HD) * inv_n
    oc = o - mu
    var = _head_sums(oc * oc, RET_HD) * inv_n
    y_ref[...] = (oc * lax.rsqrt(var + LN_EPS) * ln_g + ln_b) * _silu(gate)


def _ret(pc, pos_col, freq_full, lg_full, lg_stack, vec):
    n = pc.shape[0]
    tile = MIX_TILE
    n_stack = RET_HEADS * RET_CHUNK
    scratch = [pltpu.VMEM((RET_W, RET_W), F32), pltpu.VMEM((n_stack, n_stack), F32),
               pltpu.VMEM((n_stack, RET_W), F32), pltpu.VMEM((n_stack, RET_W), F32)]
    scratch += [pltpu.VMEM((tile, RET_W), F32) for _ in range(4)]
    return pl.pallas_call(
        _ret_kernel,
        grid=(n // tile,),
        in_specs=[_row_spec(tile, 1024), _row_spec(tile, 1), _const_spec((1, RET_W)), _const_spec((1, RET_W)),
                  _const_spec((1, n_stack)), _const_spec((8, RET_W))],
        out_specs=_row_spec(tile, RET_W),
        out_shape=jax.ShapeDtypeStruct((n, RET_W), F32),
        scratch_shapes=scratch,
        compiler_params=_params(),
        name="retention",
    )(pc, pos_col, freq_full, lg_full, lg_stack, vec)


def _outproj_kernel(alpha, ya_ref, yb_ref, yc_ref, x_ref, w_ref, gt_ref, g_ref, b_ref, o_ref):
    mix = (_mm(ya_ref[...], w_ref[0:256, :]) + _mm(yb_ref[...], w_ref[256:768, :])
           + _mm(yc_ref[...], w_ref[768:1024, :]))
    o_ref[...] = _layer_norm_rows(alpha * x_ref[...] + (1.0 + gt_ref[...]) * mix, g_ref[...], b_ref[...])


def _outproj(alpha, ya, yb, yc, x2, w_out, gt, ln_g, ln_b):
    n, d = x2.shape
    tile = PROJ_TILE
    vecs = _const_spec((1, d))
    return pl.pallas_call(
        functools.partial(_outproj_kernel, alpha),
        grid=(n // tile,),
        in_specs=[_row_spec(tile, 256), _row_spec(tile, 512), _row_spec(tile, 256), _row_spec(tile, d),
                  _const_spec((d, d)), vecs, vecs, vecs],
        out_specs=_row_spec(tile, d),
        out_shape=jax.ShapeDtypeStruct((n, d), F32),
        compiler_params=_params(),
        name="outproj_ln",
    )(ya, yb, yc, x2, w_out, gt, ln_g, ln_b)


def _router_kernel(x_ref, sc_ref, sh_ref, wt_ref, b_ref, idx_ref, rank_ref, gate_ref, cnt_ref, run_ref):
    @pl.when(pl.program_id(0) == 0)
    def _():
        run_ref[...] = jnp.zeros_like(run_ref)

    tile = x_ref.shape[0]
    h = x_ref[...] * (1.0 + sc_ref[...]) + sh_ref[...]
    logits = _mm3_nt(wt_ref[...], h) + b_ref[:, 0:1]
    e_iota = _iota2((N_EXPERTS, tile), 0).astype(F32)
    vals, hots, idxs = [], [], []
    for _ in range(TOP_K):
        m = jnp.max(logits, axis=0, keepdims=True)
        idx = jnp.min(jnp.where(logits == m, e_iota, float(N_EXPERTS)), axis=0, keepdims=True)
        hot = e_iota == idx
        logits = jnp.where(hot, -jnp.inf, logits)
        vals.append(m)
        idxs.append(idx.astype(I32))
        hots.append(hot)
    exps = [jnp.exp(vk - vals[0]) for vk in vals]
    denom = exps[0] + exps[1] + exps[2] + exps[3]
    hot_all = jnp.where(hots[0] | hots[1] | hots[2] | hots[3], 1.0, 0.0)
    before = jnp.where(_iota2((tile, tile), 0) < _iota2((tile, tile), 1), 1.0, 0.0).astype(BF16)
    seen = _dot(hot_all.astype(BF16), before) + run_ref[:, 0:1]
    for kk in range(TOP_K):
        idx_ref[kk:kk + 1, :] = idxs[kk]
        rank_ref[kk:kk + 1, :] = jnp.sum(jnp.where(hots[kk], seen, 0.0), axis=0, keepdims=True).astype(I32)
        gate_ref[kk:kk + 1, :] = exps[kk] / denom
    run_ref[...] = run_ref[...] + jnp.sum(hot_all, axis=1, keepdims=True)
    cnt_ref[...] = run_ref[...].astype(I32)


def _router(x1, sc, sh, router_wt, router_b):
    n, d = x1.shape
    tile = PROJ_TILE
    lane_spec = pl.BlockSpec((TOP_K, tile), lambda i: (0, i))
    return pl.pallas_call(
        _router_kernel,
        grid=(n // tile,),
        in_specs=[_row_spec(tile, d), _const_spec((1, d)), _const_spec((1, d)),
                  _const_spec((N_EXPERTS, d)), _const_spec((N_EXPERTS, 128))],
        out_specs=[lane_spec, lane_spec, lane_spec, _const_spec((N_EXPERTS, 128))],
        out_shape=[jax.ShapeDtypeStruct((TOP_K, n), I32), jax.ShapeDtypeStruct((TOP_K, n), I32),
                   jax.ShapeDtypeStruct((TOP_K, n), F32), jax.ShapeDtypeStruct((N_EXPERTS, 128), I32)],
        scratch_shapes=[pltpu.VMEM((N_EXPERTS, 128), F32)],
        compiler_params=_params(),
        name="router",
    )(x1, sc, sh, router_wt, router_b)


ROW_SUB = D_MODEL // 128


def _row_copy(src, src_row, dst, dst_row, sem):
    s0 = pl.multiple_of(src_row * ROW_SUB, ROW_SUB)
    d0 = pl.multiple_of(dst_row * ROW_SUB, ROW_SUB)
    return pltpu.make_async_copy(src.at[pl.ds(s0, ROW_SUB), :], dst.at[pl.ds(d0, ROW_SUB), :], sem)


def _to_row_tiles(ref, x):
    rows = x.shape[0]
    for s in range(ROW_SUB):
        ref[pl.ds(s, rows, stride=ROW_SUB), :] = x[:, s * 128:(s + 1) * 128]


def _from_row_tiles(ref, rows):
    return jnp.concatenate([ref[pl.ds(s, rows, stride=ROW_SUB), :] for s in range(ROW_SUB)], axis=1)


def _dest_kernel(pstart_ref, idx_ref, rank_ref, dest_ref):
    idx = idx_ref[...]
    dest = rank_ref[...]
    for e in range(N_EXPERTS):
        dest = dest + jnp.where(idx == e, pstart_ref[e], 0)
    dest_ref[...] = dest


def _dest(pstart, idx, rank):
    grid_spec = pltpu.PrefetchScalarGridSpec(
        num_scalar_prefetch=1, grid=(1,),
        in_specs=[pl.BlockSpec(idx.shape, lambda i, *_: (0, 0)), pl.BlockSpec(idx.shape, lambda i, *_: (0, 0))],
        out_specs=pl.BlockSpec(idx.shape, lambda i, *_: (0, 0)))
    return pl.pallas_call(_dest_kernel, grid_spec=grid_spec, out_shape=jax.ShapeDtypeStruct(idx.shape, I32),
                          compiler_params=_params(), name="moe_dest")(pstart, idx, rank)


def _dispatch_kernel(n_tok, n_blocks, dest_ref, pend_ref, x_ref, sc_ref, sh_ref, xs_ref, h_ref, zero_ref, sem):
    i = pl.program_id(0)
    tile = x_ref.shape[0]
    n_used = pend_ref[N_EXPERTS - 1] // MOE_ROWS

    def zero_block(b):
        r0 = pl.multiple_of(b * (MOE_ROWS * ROW_SUB), MOE_ROWS * ROW_SUB)
        return pltpu.make_async_copy(zero_ref, xs_ref.at[pl.ds(r0, MOE_ROWS * ROW_SUB), :], sem)

    def is_last_of_expert(b):
        hit = b < 0
        for e in range(N_EXPERTS):
            prev = pend_ref[e - 1] if e > 0 else 0
            hit = hit | ((pend_ref[e] > prev) & (pend_ref[e] == (b + 1) * MOE_ROWS))
        return hit

    @pl.when(i == 0)
    def _():
        zero_ref[...] = jnp.zeros_like(zero_ref)

        def start(b, c):
            @pl.when((b >= n_used) | is_last_of_expert(b))
            def _():
                zero_block(b).start()
            return c

        def wait(b, c):
            @pl.when((b >= n_used) | is_last_of_expert(b))
            def _():
                zero_block(b).wait()
            return c

        lax.fori_loop(0, n_blocks, start, 0)
        lax.fori_loop(0, n_blocks, wait, 0)

    _to_row_tiles(h_ref, x_ref[...] * (1.0 + sc_ref[...]) + sh_ref[...])
    base = i * tile

    def start_rows(t, c):
        for kk in range(TOP_K):
            _row_copy(h_ref, t, xs_ref, dest_ref[kk * n_tok + base + t], sem).start()
        return c

    def wait_rows(t, c):
        for kk in range(TOP_K):
            _row_copy(h_ref, t, xs_ref, dest_ref[kk * n_tok + base + t], sem).wait()
        return c

    lax.fori_loop(0, tile, start_rows, 0)
    lax.fori_loop(0, tile, wait_rows, 0)


def _dispatch(dest_flat, pend, x1, sc, sh, cap):
    n, d = x1.shape
    tile = MOE_TILE
    grid_spec = pltpu.PrefetchScalarGridSpec(
        num_scalar_prefetch=2,
        grid=(n // tile,),
        in_specs=[pl.BlockSpec((tile, d), lambda i, *_: (i, 0)),
                  pl.BlockSpec((1, d), lambda i, *_: (0, 0)),
                  pl.BlockSpec((1, d), lambda i, *_: (0, 0))],
        out_specs=pl.BlockSpec(memory_space=pl.ANY),
        scratch_shapes=[pltpu.VMEM((tile * ROW_SUB, 128), F32), pltpu.VMEM((MOE_ROWS * ROW_SUB, 128), F32),
                        pltpu.SemaphoreType.DMA],
    )
    return pl.pallas_call(
        functools.partial(_dispatch_kernel, n, cap // MOE_ROWS),
        grid_spec=grid_spec,
        out_shape=jax.ShapeDtypeStruct((cap * ROW_SUB, 128), F32),
        compiler_params=_params(),
        name="moe_dispatch",
    )(dest_flat, pend, x1, sc, sh)


def _expert_kernel(be_ref, nb_ref, xs_ref, wu_ref, bu_ref, wd_ref, bd_ref, ys_ref, wu_s, wd_s):
    b = pl.program_id(0)

    @pl.when(b >= nb_ref[0])
    def _():
        ys_ref[...] = jnp.zeros_like(ys_ref)

    @pl.when(b < nb_ref[0])
    def _():
        prev = be_ref[jnp.maximum(b - 1, 0)]

        @pl.when((b == 0) | (be_ref[b] != prev))
        def _():
            wu_s[...] = wu_ref[0, 0].astype(BF16)
            wd_s[...] = wd_ref[0, 0].astype(BF16)

        up = _dot(_from_row_tiles(xs_ref, MOE_ROWS).astype(BF16), wu_s[...]) + bu_ref[0, 0]
        glu = jnp.minimum(up[:, 0:D_FF], SWIGLU_LIMIT)
        lin = jnp.clip(up[:, D_FF:2 * D_FF], -SWIGLU_LIMIT, SWIGLU_LIMIT)
        act = glu * _sigmoid(SWIGLU_ALPHA * glu) * (lin + 1.0)
        _to_row_tiles(ys_ref, _dot(act.astype(BF16), wd_s[...]) + bd_ref[0, 0])


def _experts(layer, block_e, n_used, xs, w_up, b_up, w_down, b_down):
    d = D_MODEL
    depth = w_up.shape[0]
    n_blocks = xs.shape[0] // (MOE_ROWS * ROW_SUB)

    def blk(b, be, nb):
        return jnp.minimum(b, nb[0] - 1)

    def wmap(b, be, nb):
        return (layer, be[blk(b, be, nb)], 0, 0)

    grid_spec = pltpu.PrefetchScalarGridSpec(
        num_scalar_prefetch=2,
        grid=(n_blocks,),
        in_specs=[pl.BlockSpec((MOE_ROWS * ROW_SUB, 128), lambda b, be, nb: (blk(b, be, nb), 0)),
                  pl.BlockSpec((1, 1, d, 2 * D_FF), wmap), pl.BlockSpec((1, 1, 1, 2 * D_FF), wmap),
                  pl.BlockSpec((1, 1, D_FF, d), wmap), pl.BlockSpec((1, 1, 1, d), wmap)],
        out_specs=pl.BlockSpec((MOE_ROWS * ROW_SUB, 128), lambda b, be, nb: (b, 0)),
        scratch_shapes=[pltpu.VMEM((d, 2 * D_FF), BF16), pltpu.VMEM((D_FF, d), BF16)],
    )
    return pl.pallas_call(
        _expert_kernel,
        grid_spec=grid_spec,
        out_shape=jax.ShapeDtypeStruct(xs.shape, F32),
        compiler_params=_params(),
        name="moe_experts",
    )(block_e, n_used, xs, w_up, b_up.reshape(depth, N_EXPERTS, 1, 2 * D_FF), w_down,
      b_down.reshape(depth, N_EXPERTS, 1, d))


def _combine_kernel(alpha, n_tok, dest_ref, ys_ref, gates_ref, x_ref, gt_ref, g_ref, b_ref, o_ref, buf_ref, sem):
    i = pl.program_id(0)
    tile = x_ref.shape[0]
    base = i * tile

    def start_rows(t, c):
        for kk in range(TOP_K):
            _row_copy(ys_ref, dest_ref[kk * n_tok + base + t], buf_ref.at[kk], t, sem).start()
        return c

    def wait_rows(t, c):
        for kk in range(TOP_K):
            _row_copy(ys_ref, dest_ref[kk * n_tok + base + t], buf_ref.at[kk], t, sem).wait()
        return c

    lax.fori_loop(0, tile, start_rows, 0)
    lax.fori_loop(0, tile, wait_rows, 0)
    eye = jnp.where(_iota2((tile, tile), 0) == _iota2((tile, tile), 1), 1.0, 0.0).astype(BF16)
    gates = _sel_mm_nt(eye, jnp.concatenate([gates_ref[...], jnp.zeros((8 - TOP_K, tile), F32)], axis=0))
    ffn = _from_row_tiles(buf_ref.at[0], tile) * gates[:, 0:1]
    for kk in range(1, TOP_K):
        ffn = ffn + _from_row_tiles(buf_ref.at[kk], tile) * gates[:, kk:kk + 1]
    o_ref[...] = _layer_norm_rows(alpha * x_ref[...] + (1.0 + gt_ref[...]) * ffn, g_ref[...], b_ref[...])


def _combine(alpha, dest_flat, ys, gates, x1, gt, ln_g, ln_b):
    n, d = x1.shape
    tile = MOE_TILE
    vec = pl.BlockSpec((1, d), lambda i, *_: (0, 0))
    grid_spec = pltpu.PrefetchScalarGridSpec(
        num_scalar_prefetch=1,
        grid=(n // tile,),
        in_specs=[pl.BlockSpec(memory_space=pl.ANY),
                  pl.BlockSpec((TOP_K, tile), lambda i, *_: (0, i)),
                  pl.BlockSpec((tile, d), lambda i, *_: (i, 0)), vec, vec, vec],
        out_specs=pl.BlockSpec((tile, d), lambda i, *_: (i, 0)),
        scratch_shapes=[pltpu.VMEM((TOP_K, tile * ROW_SUB, 128), F32), pltpu.SemaphoreType.DMA],
    )
    return pl.pallas_call(
        functools.partial(_combine_kernel, alpha, n),
        grid_spec=grid_spec,
        out_shape=jax.ShapeDtypeStruct((n, d), F32),
        compiler_params=_params(),
        name="moe_combine_ln",
    )(dest_flat, ys, gates, x1, gt, ln_g, ln_b)


def _pad_rows(w, rows, offset=0):
    out = jnp.zeros((rows, w.shape[1]), w.dtype)
    return out.at[offset:offset + w.shape[0]].set(w)


def _moe(layer, alpha, x1, sc, sh, gt, router_w, router_b, w_up, b_up, w_down, b_down, ln_g, ln_b):
    n, d = x1.shape
    router_bias = jnp.broadcast_to(router_b[:, None], (N_EXPERTS, 128))
    idx, rank, gates, counts = _router(x1, sc, sh, router_w.T, router_bias)
    counts = counts[:, 0]
    padded = (counts + MOE_ROWS - 1) // MOE_ROWS * MOE_ROWS
    pend = jnp.cumsum(padded).astype(I32)
    pstart = pend - padded
    dest = _dest(pstart, idx, rank).reshape(TOP_K * n)
    n_blocks = -(-(n * TOP_K + N_EXPERTS * (MOE_ROWS - 1)) // MOE_ROWS)
    block_row = jnp.arange(n_blocks, dtype=I32)[:, None] * MOE_ROWS
    block_e = jnp.minimum(jnp.sum((pend[None, :] <= block_row).astype(I32), axis=1), N_EXPERTS - 1)
    n_used = (pend[-1:] // MOE_ROWS).astype(I32)
    xs = _dispatch(dest, pend, x1, sc, sh, n_blocks * MOE_ROWS)
    ys = _experts(layer, block_e, n_used, xs, w_up, b_up, w_down, b_down)
    return _combine(alpha, dest, ys, gates, x1, gt, ln_g, ln_b)


def kernel(x, c, positions, ada_w, ada_b, w_in, w_in_vres, tshift_mu, tshift_mu_vres, rwkv_w0, rwkv_w2, rwkv_a0, rwkv_a2, rwkv_g2, rwkv_kk, rwkv_ka, rwkv_rk, rwkv_ln_g, rwkv_ln_b, rwkv_v0, rwkv_v2, gdn_conv_w, gdn_a_log, gdn_dt_bias, gdn_norm_g, ret_norm_g, ret_norm_b, w_out, ln1_g, ln1_b, router_w, router_b, exp_w_up, exp_b_up, exp_w_down, exp_b_down, ln2_g, ln2_b):
    bsz, seq, d = x.shape
    assert bsz == 1 and d == D_MODEL and seq % MIX_TILE == 0
    depth = ada_w.shape[0]
    alpha = (2 * depth) ** 0.25
    n = seq
    x2 = x.reshape(n, d)
    ada = _ada(c, ada_w, ada_b)

    half = RET_HD // 2
    inv_freq = ROPE_BASE ** (-jnp.arange(half, dtype=F32) / half)
    freq_full = jnp.tile(inv_freq, 2 * RET_HEADS)[None, :]
    log_gamma = jnp.log1p(-jnp.exp2(-5.0 - jnp.arange(RET_HEADS, dtype=F32)))
    lg_full = jnp.repeat(log_gamma, RET_HD)[None, :]
    lg_stack = jnp.repeat(log_gamma, RET_CHUNK)[None, :]
    pos_col = positions.reshape(n, 1)

    v_first = None
    for l in range(depth):
        sh_mix, sc_mix, gt_mix, sh_ffn, sc_ffn, gt_ffn = (ada[l, :, i * d:(i + 1) * d] for i in range(6))
        w_l = w_in[l]
        zeros32 = jnp.zeros((d, 32), F32)
        vres_w = zeros32 if l == 0 else w_in_vres[l - 1]
        x_cols = jnp.concatenate([vres_w, w_l[:, 3072:3080], jnp.zeros((d, 88), F32)], axis=1)
        w_cat = jnp.concatenate([w_l[:, 0:1024], w_l[:, 1024:3072], w_l[:, 3080:4104], x_cols], axis=1).astype(BF16)
        mu_a = tshift_mu[l][None, :]
        mu_v = jnp.zeros((32,), F32) if l == 0 else tshift_mu_vres[l - 1]
        mu_x = jnp.concatenate([mu_v, jnp.zeros((96,), F32)])[None, :]
        pa, pb, pc, px = _inproj(x2, sc_mix, sh_mix, w_cat, mu_a, mu_x)

        v0 = jnp.zeros((RWKV_W,), F32) if l == 0 else rwkv_v0[l - 1]
        vec_a = jnp.stack([rwkv_w0[l], rwkv_a0[l], rwkv_kk[l], rwkv_ka[l], rwkv_rk[l].reshape(RWKV_W),
                           rwkv_ln_g[l], rwkv_ln_b[l], v0])
        w2p = _pad_rows(rwkv_w2[l], 128, 0).astype(BF16)
        a2p = _pad_rows(rwkv_a2[l], 128, 64).astype(BF16)
        g2 = rwkv_g2[l].astype(BF16)
        if l == 0:
            y_a, v_first = _rwkv(pa, None, None, vec_a, w2p, a2p, g2, None)
        else:
            v2p = _pad_rows(rwkv_v2[l - 1], 128, X_VRES).astype(BF16)
            y_a = _rwkv(pa, px, v_first, vec_a, w2p, a2p, g2, v2p)

        vec_b = jnp.concatenate([jnp.tile(gdn_norm_g[l], GDN_HEADS)[None, :], jnp.zeros((7, GDN_W), F32)])
        lane_pad = lambda t: jnp.concatenate([jnp.zeros((X_GA,), F32), t, jnp.zeros((128 - X_GA - GDN_HEADS,), F32)])
        vec_x = jnp.concatenate([jnp.stack([lane_pad(gdn_a_log[l]), lane_pad(gdn_dt_bias[l])]),
                                 jnp.zeros((6, 128), F32)])
        y_b = _gdn(pb, px, gdn_conv_w[l], vec_b, vec_x)

        vec_c = jnp.concatenate([jnp.stack([ret_norm_g[l], ret_norm_b[l]]), jnp.zeros((6, RET_W), F32)])
        y_c = _ret(pc, pos_col, freq_full, lg_full, lg_stack, vec_c)

        x1 = _outproj(alpha, y_a, y_b, y_c, x2, w_out[l].astype(BF16), gt_mix, ln1_g[l][None, :], ln1_b[l][None, :])
        x2 = _moe(l, alpha, x1, sc_ffn, sh_ffn, gt_ffn, router_w[l], router_b[l], exp_w_up, exp_b_up,
                  exp_w_down, exp_b_down, ln2_g[l][None, :], ln2_b[l][None, :])
    return x2.reshape(bsz, seq, d)
```

```python
import functools

import jax
import jax.numpy as jnp
from jax import lax
from jax.experimental import pallas as pl
from jax.experimental.pallas import tpu as pltpu

F32 = jnp.float32
BF16 = jnp.bfloat16
I32 = jnp.int32

D_MODEL = 1024
RWKV_HEADS, RWKV_HD, RWKV_W = 4, 64, 256
RWKV_GN_EPS = 64e-5
GDN_HEADS, GDN_HD, GDN_W = 4, 128, 512
GDN_CONV = 4
RET_HEADS, RET_HD, RET_W = 4, 64, 256
RET_CHUNK = 128
ROPE_BASE = 10000.0
N_EXPERTS, TOP_K, D_FF = 32, 4, 1024
SWIGLU_LIMIT, SWIGLU_ALPHA = 7.0, 1.702
LN_EPS, RMS_EPS = 1e-5, 1e-6

CHUNK = 64
STACK = 4 * CHUNK
MIX_TILE = 512
PROJ_TILE = 256
MOE_ROWS = 256
MOE_TILE = 256
NEG_BIG = -1e30
VMEM_LIMIT = 56 * 1024 * 1024

COL_A, COL_B, COL_C, COL_X = 0, 1024, 3072, 4096
N_PROJ = 4224
X_VRES, X_GA, X_GB = 0, 32, 36


def _dot(a, b):
    return jnp.dot(a, b, preferred_element_type=F32)


def _mm(a, b):
    return _dot(a.astype(BF16), b.astype(BF16))


def _mm_nt(a, b):
    return lax.dot_general(a.astype(BF16), b.astype(BF16), (((1,), (1,)), ((), ())),
                           preferred_element_type=F32)


def _split3(x):
    x1 = x.astype(BF16)
    r1 = x - x1.astype(F32)
    x2 = r1.astype(BF16)
    x3 = (r1 - x2.astype(F32)).astype(BF16)
    return x1, x2, x3


def _sel_mm(sel, x):
    x1, x2, x3 = _split3(x)
    return _dot(sel, x1) + (_dot(sel, x2) + _dot(sel, x3))


def _sel_mm_nt(sel, x):
    dn = (((1,), (1,)), ((), ()))
    x1, x2, x3 = _split3(x)
    d = lambda a, b: lax.dot_general(a, b, dn, preferred_element_type=F32)
    return d(sel, x1) + (d(sel, x2) + d(sel, x3))


def _mm3(a, b):
    a1 = a.astype(BF16)
    a2 = (a - a1.astype(F32)).astype(BF16)
    b1 = b.astype(BF16)
    b2 = (b - b1.astype(F32)).astype(BF16)
    return _dot(a1, b1) + (_dot(a1, b2) + _dot(a2, b1))


def _mm3_nt(a, b):
    dn = (((1,), (1,)), ((), ()))
    d = lambda x, y: lax.dot_general(x, y, dn, preferred_element_type=F32)
    a1 = a.astype(BF16)
    a2 = (a - a1.astype(F32)).astype(BF16)
    b1 = b.astype(BF16)
    b2 = (b - b1.astype(F32)).astype(BF16)
    return d(a1, b1) + (d(a1, b2) + d(a2, b1))


def _iota2(shape, axis):
    return lax.broadcasted_iota(I32, shape, axis)


def _softplus(x):
    return jnp.maximum(x, 0.0) + jnp.log1p(jnp.exp(-jnp.abs(x)))


def _sigmoid(x):
    return jax.nn.sigmoid(x)


def _silu(x):
    return x * jax.nn.sigmoid(x)


def _same_block(n_rows, n_cols, block):
    r = _iota2((n_rows, n_cols), 0)
    c = _iota2((n_rows, n_cols), 1)
    return (r // block) == (c // block), r, c


def _block_cumsum_mat(n, block):
    same, r, c = _same_block(n, n, block)
    return jnp.where(same & (c <= r), 1.0, 0.0).astype(BF16)


def _head_masks(width, head_dim):
    lane = _iota2((1, width), 1)
    return [jnp.where((lane // head_dim) == h, 1.0, 0.0).astype(F32) for h in range(width // head_dim)]


def _stack_heads(x, masks):
    return jnp.concatenate([x * m for m in masks], axis=0)


def _fold_heads(x4, n_heads):
    c = x4.shape[0] // n_heads
    out = x4[0:c]
    for h in range(1, n_heads):
        out = out + x4[h * c:(h + 1) * c]
    return out


def _chunk_last_row(x, chunk):
    rows, width = x.shape
    x3 = x.reshape(rows // chunk, chunk, width)
    return jnp.broadcast_to(x3[:, chunk - 1:chunk, :], x3.shape).reshape(rows, width)


def _unit_lower_inverses(ms):
    n = ms[0].shape[0]
    eye = jnp.where(_iota2((n, n), 0) == _iota2((n, n), 1), 1.0, 0.0).astype(F32)
    ts = [eye - m for m in ms]
    ps = [m.astype(BF16) for m in ms]
    for step in range(5):
        ps = [_dot(p, p).astype(BF16) for p in ps]
        ts = [t + _dot(t.astype(BF16), p) for t, p in zip(ts, ps)]
    return ts


def _layer_norm_rows(x, g, b):
    mu = jnp.mean(x, axis=-1, keepdims=True)
    xc = x - mu
    var = jnp.mean(xc * xc, axis=-1, keepdims=True)
    return xc * lax.rsqrt(var + LN_EPS) * g + b


def _shift_down(x, prev8, d):
    rolled = pltpu.roll(x, d, axis=0)
    head = pltpu.roll(jnp.concatenate([prev8, x[0:8]], axis=0), d, axis=0)[8:16]
    return jnp.concatenate([head, rolled[8:]], axis=0)


def _head_sums(x, head_dim):
    lane = _iota2((1, 128), 1)
    parts = []
    for j in range(x.shape[1] // 128):
        slab = x[:, j * 128:(j + 1) * 128]
        if head_dim == 128:
            parts.append(jnp.broadcast_to(jnp.sum(slab, axis=-1, keepdims=True), slab.shape))
        else:
            lo = lane < 64
            s_lo = jnp.sum(jnp.where(lo, slab, 0.0), axis=-1, keepdims=True)
            s_hi = jnp.sum(jnp.where(lo, 0.0, slab), axis=-1, keepdims=True)
            parts.append(jnp.where(lo, s_lo, s_hi))
    return jnp.concatenate(parts, axis=1)


def _params(n_grid_axes=1, vmem=VMEM_LIMIT):
    return pltpu.CompilerParams(dimension_semantics=("arbitrary",) * n_grid_axes, vmem_limit_bytes=vmem)


def _row_spec(tile, width):
    return pl.BlockSpec((tile, width), lambda i: (i, 0))


def _const_spec(shape):
    zeros = (0,) * len(shape)
    return pl.BlockSpec(shape, lambda i: zeros)


def _ada_kernel(c_ref, w_ref, b_ref, o_ref):
    c = c_ref[...]
    o_ref[0] = _mm3(_silu(c), w_ref[0]) + b_ref[0]


def _ada(c, ada_w, ada_b):
    depth, d, d6 = ada_w.shape
    blk = 1024
    c8 = jnp.broadcast_to(c, (8, d))
    out = pl.pallas_call(
        _ada_kernel,
        grid=(depth, d6 // blk),
        in_specs=[pl.BlockSpec((8, d), lambda l, j: (0, 0)),
                  pl.BlockSpec((1, d, blk), lambda l, j: (l, 0, j)),
                  pl.BlockSpec((1, 1, blk), lambda l, j: (l, 0, j))],
        out_specs=pl.BlockSpec((1, 8, blk), lambda l, j: (l, 0, j)),
        out_shape=jax.ShapeDtypeStruct((depth, 8, d6), F32),
        compiler_params=_params(2),
        name="ada",
    )(c8, ada_w, ada_b.reshape(depth, 1, d6))
    return out[:, 0:1, :]


def _inproj_kernel(x_ref, sc_ref, sh_ref, w_ref, mua_ref, mux_ref, pa_ref, pb_ref, pc_ref, px_ref, carry_ref):
    @pl.when(pl.program_id(0) == 0)
    def _():
        carry_ref[...] = jnp.zeros_like(carry_ref)

    tile = x_ref.shape[0]
    h = (x_ref[...] * (1.0 + sc_ref[...]) + sh_ref[...]).astype(BF16)
    pa = _dot(h, w_ref[:, COL_A:COL_B])
    px = _dot(h, w_ref[:, COL_X:N_PROJ])
    pb_ref[...] = _dot(h, w_ref[:, COL_B:COL_C])
    pc_ref[...] = _dot(h, w_ref[:, COL_C:COL_X])
    prev_a = _shift_down(pa, carry_ref[:, 0:1024], 1)
    prev_x = _shift_down(px, carry_ref[:, 1024:1152], 1)
    carry_ref[:, 0:1024] = pa[tile - 8:tile, :]
    carry_ref[:, 1024:1152] = px[tile - 8:tile, :]
    pa_ref[...] = pa + (prev_a - pa) * mua_ref[...]
    px_ref[...] = px + (prev_x - px) * mux_ref[...]


def _inproj(x2, sc, sh, w_cat, mu_a, mu_x):
    n, d = x2.shape
    tile = PROJ_TILE
    return pl.pallas_call(
        _inproj_kernel,
        grid=(n // tile,),
        in_specs=[_row_spec(tile, d), _const_spec((1, d)), _const_spec((1, d)),
                  _const_spec((d, N_PROJ)), _const_spec((1, 1024)), _const_spec((1, 128))],
        out_specs=[_row_spec(tile, 1024), _row_spec(tile, 2048), _row_spec(tile, 1024), _row_spec(tile, 128)],
        out_shape=[jax.ShapeDtypeStruct((n, 1024), F32), jax.ShapeDtypeStruct((n, 2048), F32),
                   jax.ShapeDtypeStruct((n, 1024), F32), jax.ShapeDtypeStruct((n, 128), F32)],
        scratch_shapes=[pltpu.VMEM((8, 1152), F32)],
        compiler_params=_params(),
        name="inproj",
    )(x2, sc, sh, w_cat, mu_a, mu_x)


def _rwkv_kernel(has_vres, *refs):
    if has_vres:
        (pa_ref, px_ref, vf_ref, vec_ref, w2_ref, a2_ref, g2_ref, v2_ref,
         y_ref, state_ref, rt_s, at_s, kt_s, bt_s, kh_s, bh_s, v_s, gc_s, y_s) = refs
    else:
        (pa_ref, vec_ref, w2_ref, a2_ref, g2_ref,
         y_ref, vout_ref, state_ref, rt_s, at_s, kt_s, bt_s, kh_s, bh_s, v_s, gc_s, y_s) = refs

    @pl.when(pl.program_id(0) == 0)
    def _():
        state_ref[...] = jnp.zeros_like(state_ref)

    tile = pa_ref.shape[0]
    w0, a0, k_k, k_a, r_k, ln_g, ln_b, v0 = (vec_ref[i:i + 1, :] for i in range(8))
    r = pa_ref[:, 0:256]
    k = pa_ref[:, 256:512]
    v = pa_ref[:, 512:768]
    wa = pa_ref[:, 768:896]
    g_lo = pa_ref[:, 896:1024]

    w_raw = -_softplus(-(w0 + _mm(jnp.tanh(wa), w2_ref[...]))) - 0.5
    log_decay = -jnp.exp(w_raw)
    a = _sigmoid(a0 + _mm(wa, a2_ref[...]))
    gate = _mm(_sigmoid(g_lo), g2_ref[...])
    if has_vres:
        v = v + (vf_ref[...] - v) * _sigmoid(v0 + _mm(px_ref[...], v2_ref[...]))
    else:
        vout_ref[...] = v

    kk = k * k_k
    kk = kk * lax.rsqrt(_head_sums(kk * kk, RWKV_HD) + 1e-12)
    k = k * (1.0 + (a - 1.0) * k_a)
    bonus = _head_sums(r * k * r_k, RWKV_HD) * v

    cl = _sel_mm(_block_cumsum_mat(tile, CHUNK), log_decay)
    cl_end = _chunk_last_row(cl, CHUNK)
    e_inv = jnp.exp(-cl)
    e_end = jnp.exp(cl_end - cl)
    kka = kk * a
    rt_s[...] = r * jnp.exp(cl)
    at_s[...] = -kk * jnp.exp(cl - log_decay)
    kt_s[...] = k * e_inv
    bt_s[...] = kka * e_inv
    kh_s[...] = k * e_end
    bh_s[...] = kka * e_end
    v_s[...] = v
    gc_s[...] = jnp.exp(cl_end)

    masks = _head_masks(RWKV_W, RWKV_HD)
    same, row, col = _same_block(STACK, STACK, CHUNK)
    strict = same & (row > col)
    incl = same & (row >= col)
    same_head = (_iota2((RWKV_W, RWKV_W), 0) // RWKV_HD) == (_iota2((RWKV_W, RWKV_W), 1) // RWKV_HD)

    chunks = range(tile // CHUNK)
    rows = [slice(c * CHUNK, (c + 1) * CHUNK) for c in chunks]
    at4 = [_stack_heads(at_s[r, :], masks).astype(BF16) for r in rows]
    rt4 = [_stack_heads(rt_s[r, :], masks).astype(BF16) for r in rows]
    kb4 = [jnp.concatenate([_stack_heads(kt_s[r, :], masks), _stack_heads(bt_s[r, :], masks)], axis=0).astype(BF16)
           for r in rows]
    v4 = [_stack_heads(v_s[r, :], masks).astype(BF16) for r in rows]
    s_a = [_mm_nt(at4[c], kb4[c]) for c in chunks]
    s_r = [_mm_nt(rt4[c], kb4[c]) for c in chunks]
    a_ak = [jnp.where(strict, s_a[c][:, 0:STACK], 0.0).astype(BF16) for c in chunks]
    a_rk = [jnp.where(incl, s_r[c][:, 0:STACK], 0.0).astype(BF16) for c in chunks]
    a_rb = [jnp.where(incl, s_r[c][:, STACK:2 * STACK], 0.0).astype(BF16) for c in chunks]
    t_inv = _unit_lower_inverses([jnp.where(strict, -s_a[c][:, STACK:2 * STACK], 0.0) for c in chunks])
    w4 = [_mm(t_inv[c], at4[c]).astype(BF16) for c in chunks]
    akv = [_mm(a_ak[c], v4[c]) for c in chunks]
    u0 = [_mm(t_inv[c], akv[c]) for c in chunks]
    yv = [_mm(a_rk[c], v4[c]) for c in chunks]

    state = state_ref[...]
    for c in chunks:
        r = rows[c]
        u4 = _mm_nt(w4[c], state) + u0[c]
        y4 = yv[c] + _mm(a_rb[c], u4)
        y_s[r, :] = _mm_nt(rt_s[r, :], state) + _fold_heads(y4, RWKV_HEADS)
        zt = jnp.concatenate([v_s[r, :], _fold_heads(u4, RWKV_HEADS)], axis=0).T
        kb = jnp.concatenate([kh_s[r, :], bh_s[r, :]], axis=0)
        state = state * gc_s[c * CHUNK:c * CHUNK + 1, :] + jnp.where(same_head, _mm(zt, kb), 0.0)
    state_ref[...] = state

    y = y_s[...]
    inv_n = 1.0 / RWKV_HD
    mu = _head_sums(y, RWKV_HD) * inv_n
    yc = y - mu
    var = _head_sums(yc * yc, RWKV_HD) * inv_n
    yn = yc * lax.rsqrt(var + RWKV_GN_EPS) * ln_g + ln_b
    y_ref[...] = (yn + bonus) * gate


def _rwkv(pa, px, v_first, vec, w2p, a2p, g2, v2p):
    n = pa.shape[0]
    tile = MIX_TILE
    has_vres = v_first is not None
    big = [pltpu.VMEM((tile, RWKV_W), F32) for _ in range(9)]
    scratch = [pltpu.VMEM((RWKV_W, RWKV_W), F32)] + big
    wspec = _const_spec((128, RWKV_W))
    if has_vres:
        in_specs = [_row_spec(tile, 1024), _row_spec(tile, 128), _row_spec(tile, RWKV_W),
                    _const_spec((8, RWKV_W)), wspec, wspec, wspec, wspec]
        args = (pa, px, v_first, vec, w2p, a2p, g2, v2p)
        out_specs = _row_spec(tile, RWKV_W)
        out_shape = jax.ShapeDtypeStruct((n, RWKV_W), F32)
    else:
        in_specs = [_row_spec(tile, 1024), _const_spec((8, RWKV_W)), wspec, wspec, wspec]
        args = (pa, vec, w2p, a2p, g2)
        out_specs = [_row_spec(tile, RWKV_W), _row_spec(tile, RWKV_W)]
        out_shape = [jax.ShapeDtypeStruct((n, RWKV_W), F32), jax.ShapeDtypeStruct((n, RWKV_W), F32)]
    return pl.pallas_call(
        functools.partial(_rwkv_kernel, has_vres),
        grid=(n // tile,),
        in_specs=in_specs, out_specs=out_specs, out_shape=out_shape,
        scratch_shapes=scratch,
        compiler_params=_params(),
        name="rwkv7_vres" if has_vres else "rwkv7",
    )(*args)


def _gdn_kernel(pb_ref, px_ref, convw_ref, vec_ref, vecx_ref, y_ref,
                state_ref, carry_ref, q_s, qd_s, k_s, kb_s, kd_s, rhs_s, gcs_s, gl_s, o_s):
    @pl.when(pl.program_id(0) == 0)
    def _():
        state_ref[...] = jnp.zeros_like(state_ref)
        carry_ref[...] = jnp.zeros_like(carry_ref)

    tile = pb_ref.shape[0]
    width3 = 3 * GDN_W
    norm_g = vec_ref[0:1, :]

    x = pb_ref[:, 0:width3]
    prev8 = carry_ref[...]
    acc = x * convw_ref[GDN_CONV - 1:GDN_CONV, :]
    for d in range(1, GDN_CONV):
        acc = acc + _shift_down(x, prev8, d) * convw_ref[GDN_CONV - 1 - d:GDN_CONV - d, :]
    carry_ref[...] = x[tile - 8:tile, :]
    qkv = _silu(acc)
    q = qkv[:, 0:GDN_W]
    k = qkv[:, GDN_W:2 * GDN_W]
    v = qkv[:, 2 * GDN_W:width3]
    q = q * lax.rsqrt(_head_sums(q * q, GDN_HD) + 1e-12) * (GDN_HD ** -0.5)
    k = k * lax.rsqrt(_head_sums(k * k, GDN_HD) + 1e-12)

    px = px_ref[...]
    g_x = -jnp.exp(vecx_ref[0:1, :]) * _softplus(px + vecx_ref[1:2, :])
    beta_x = _sigmoid(px)
    gc_x = _sel_mm(_block_cumsum_mat(tile, CHUNK), g_x)
    gl_x = _chunk_last_row(gc_x, CHUNK)
    eg_x = jnp.exp(gc_x)
    ed_x = jnp.exp(gl_x - gc_x)
    el_x = jnp.exp(gl_x)
    for h in range(GDN_HEADS):
        hs = slice(h * GDN_HD, (h + 1) * GDN_HD)
        col = lambda t, lane0: t[:, lane0 + h:lane0 + h + 1]
        beta, eg = col(beta_x, X_GB), col(eg_x, X_GA)
        kb = k[:, hs] * beta
        q_s[:, hs] = q[:, hs]
        qd_s[:, hs] = q[:, hs] * eg
        k_s[:, hs] = k[:, hs]
        kb_s[:, hs] = kb
        kd_s[:, hs] = k[:, hs] * col(ed_x, X_GA)
        rhs_s[:, hs] = v[:, hs] * beta
        rhs_s[:, GDN_W + h * GDN_HD:GDN_W + (h + 1) * GDN_HD] = kb * eg
        gcs_s[:, hs] = jnp.broadcast_to(col(gc_x, X_GA), (tile, GDN_HD))
        gl_s[:, hs] = jnp.broadcast_to(col(el_x, X_GA), (tile, GDN_HD))

    same, rr, cc = _same_block(STACK, STACK, CHUNK)
    strict = same & (rr > cc)
    incl = same & (rr >= cc)
    lane = _iota2((STACK, GDN_HD), 1)
    pick3 = jnp.where(lane < 3, 1.0, 0.0).astype(BF16)
    row_head = [jnp.where((_iota2((1, STACK), 1) // CHUNK) == h, 1.0, 0.0).astype(F32) for h in range(GDN_HEADS)]

    def stack(x):
        return jnp.concatenate([x[:, h * GDN_HD:(h + 1) * GDN_HD] for h in range(GDN_HEADS)], axis=0)

    def unstack(y):
        return jnp.concatenate([y[h * CHUNK:(h + 1) * CHUNK] for h in range(GDN_HEADS)], axis=1)

    def per_head(fn):
        return [fn(h, slice(h * CHUNK, (h + 1) * CHUNK)) for h in range(GDN_HEADS)]

    def decay_mask(r):
        gcol = stack(gcs_s[r, :])
        col_form = jnp.concatenate([gcol, gcol], axis=1)
        g1, g2, g3 = _split3(gcol)
        pieces = jnp.where(lane == 0, g1, jnp.where(lane == 1, g2, jnp.where(lane == 2, g3, jnp.zeros_like(g1))))
        row_form = lax.dot_general(pick3, pieces, (((1,), (1,)), ((), ())), preferred_element_type=F32)
        return jnp.exp(jnp.where(incl, col_form - row_form, NEG_BIG))

    chunks = range(tile // CHUNK)
    rows = [slice(c * CHUNK, (c + 1) * CHUNK) for c in chunks]
    k_c = [stack(k_s[r, :]).astype(BF16) for r in rows]
    s_kb = [_mm_nt(stack(kb_s[r, :]), k_c[c]) for c, r in enumerate(rows)]
    s_q = [_mm_nt(stack(q_s[r, :]), k_c[c]) for c, r in enumerate(rows)]
    decay = [decay_mask(r) for r in rows]
    attn = [jnp.where(incl, s_q[c] * decay[c], 0.0).astype(BF16) for c in chunks]
    t_inv = _unit_lower_inverses([jnp.where(strict, s_kb[c] * decay[c], 0.0) for c in chunks])
    sol = [_mm(t_inv[c], jnp.concatenate([stack(rhs_s[r, 0:GDN_W]), stack(rhs_s[r, GDN_W:2 * GDN_W])], axis=1))
           for c, r in enumerate(rows)]

    states = [state_ref[h] for h in range(GDN_HEADS)]
    for c in chunks:
        r = rows[c]
        u_c, w_c = sol[c][:, 0:GDN_HD], sol[c][:, GDN_HD:2 * GDN_HD]
        qd_c = stack(qd_s[r, :])
        kd_t = stack(kd_s[r, :]).T
        v_new = u_c - jnp.concatenate(per_head(lambda h, hr: _mm(w_c[hr], states[h])), axis=0)
        o_c = jnp.concatenate(per_head(lambda h, hr: _mm(qd_c[hr], states[h])), axis=0) + _mm(attn[c], v_new)
        o_s[r, :] = unstack(o_c)
        gl = gl_s[c * CHUNK:c * CHUNK + 1, :]
        states = [states[h] * gl[:, h * GDN_HD:(h + 1) * GDN_HD] + _mm(kd_t * row_head[h], v_new)
                  for h in range(GDN_HEADS)]
    for h in range(GDN_HEADS):
        state_ref[h] = states[h]

    o = o_s[...]
    ms = _head_sums(o * o, GDN_HD) * (1.0 / GDN_HD)
    z = pb_ref[:, width3:width3 + GDN_W]
    y_ref[...] = o * lax.rsqrt(ms + RMS_EPS) * norm_g * _silu(z)


def _gdn(pb, px, conv_w, vec, vec_x):
    n = pb.shape[0]
    tile = MIX_TILE
    wide = lambda w: pltpu.VMEM((tile, w), F32)
    scratch = [pltpu.VMEM((GDN_HEADS, GDN_HD, GDN_HD), F32), pltpu.VMEM((8, 3 * GDN_W), F32),
               wide(GDN_W), wide(GDN_W), wide(GDN_W), wide(GDN_W), wide(GDN_W), wide(2 * GDN_W),
               wide(GDN_W), wide(GDN_W), wide(GDN_W)]
    return pl.pallas_call(
        _gdn_kernel,
        grid=(n // tile,),
        in_specs=[_row_spec(tile, 2048), _row_spec(tile, 128), _const_spec((GDN_CONV, 3 * GDN_W)),
                  _const_spec((8, GDN_W)), _const_spec((8, 128))],
        out_specs=_row_spec(tile, GDN_W),
        out_shape=jax.ShapeDtypeStruct((n, GDN_W), F32),
        scratch_shapes=scratch,
        compiler_params=_params(),
        name="gdn",
    )(pb, px, conv_w, vec, vec_x)


def _ret_kernel(pc_ref, pos_ref, freq_ref, lg_ref, lgs_ref, vec_ref, y_ref,
                state_ref, dmat_ref, xi_ref, zeta_ref, q_s, k_s, v_s, o_s):
    n_stack = RET_HEADS * RET_CHUNK

    @pl.when(pl.program_id(0) == 0)
    def _():
        state_ref[...] = jnp.zeros_like(state_ref)
        same, r, c = _same_block(n_stack, n_stack, RET_CHUNK)
        dist = (r - c).astype(F32)
        dmat_ref[...] = jnp.exp(jnp.where(same & (r >= c), dist * lgs_ref[...], NEG_BIG))
        idx = (_iota2((n_stack, RET_W), 0) % RET_CHUNK).astype(F32)
        xi_ref[...] = jnp.exp((idx + 1.0) * lg_ref[...])
        zeta_ref[...] = jnp.exp((RET_CHUNK - 1.0 - idx) * lg_ref[...])

    tile = pc_ref.shape[0]
    ln_g, ln_b = vec_ref[0:1, :], vec_ref[1:2, :]
    first_half = (_iota2((1, 128), 1) % RET_HD) < RET_HD // 2
    ang = pos_ref[...].astype(F32) * freq_ref[:, 0:128]
    cos_slab = jnp.cos(ang)
    sin_slab = jnp.where(first_half, -1.0, 1.0) * jnp.sin(ang)
    cos = jnp.concatenate([cos_slab] * (RET_W // 128), axis=1)
    sin_signed = jnp.concatenate([sin_slab] * (RET_W // 128), axis=1)

    def partner(t):
        slabs = [t[:, j * 128:(j + 1) * 128] for j in range(RET_W // 128)]
        return jnp.concatenate([jnp.where(first_half, pltpu.roll(sl, 128 - RET_HD // 2, axis=1),
                                          pltpu.roll(sl, RET_HD // 2, axis=1)) for sl in slabs], axis=1)

    def rotary(t):
        return t * cos + partner(t) * sin_signed

    q_s[...] = rotary(pc_ref[:, 0:RET_W])
    k_s[...] = rotary(pc_ref[:, RET_W:2 * RET_W]) * (RET_HD ** -0.5)
    v_s[...] = pc_ref[:, 2 * RET_W:3 * RET_W]
    gate = pc_ref[:, 3 * RET_W:4 * RET_W]
    masks = _head_masks(RET_W, RET_HD)
    chunk_decay = jnp.exp(float(RET_CHUNK) * lg_ref[...])

    def chunk_body(c, carry):
        rows = pl.ds(pl.multiple_of(c * RET_CHUNK, RET_CHUNK), RET_CHUNK)
        q4 = _stack_heads(q_s[rows, :], masks)
        k4 = _stack_heads(k_s[rows, :], masks)
        v4 = _stack_heads(v_s[rows, :], masks)
        scores = _mm_nt(q4, k4) * dmat_ref[...]
        state = state_ref[...]
        o4 = _mm(scores, v4) + _mm(q4 * xi_ref[...], state)
        o_s[rows, :] = _fold_heads(o4, RET_HEADS)
        state_ref[...] = state * chunk_decay + _mm((k4 * zeta_ref[...]).T, v4)
        return carry

    lax.fori_loop(0, tile // RET_CHUNK, chunk_body, 0)

    o = o_s[...]
    inv_n = 1.0 / RET_HD
    mu = _head_sums(o, RET_HD) * inv_n
    oc = o - mu
    var = _head_sums(oc * oc, RET_HD) * inv_n
    y_ref[...] = (oc * lax.rsqrt(var + LN_EPS) * ln_g + ln_b) * _silu(gate)


def _ret(pc, pos_col, freq_full, lg_full, lg_stack, vec):
    n = pc.shape[0]
    tile = MIX_TILE
    n_stack = RET_HEADS * RET_CHUNK
    scratch = [pltpu.VMEM((RET_W, RET_W), F32), pltpu.VMEM((n_stack, n_stack), F32),
               pltpu.VMEM((n_stack, RET_W), F32), pltpu.VMEM((n_stack, RET_W), F32)]
    scratch += [pltpu.VMEM((tile, RET_W), F32) for _ in range(4)]
    return pl.pallas_call(
        _ret_kernel,
        grid=(n // tile,),
        in_specs=[_row_spec(tile, 1024), _row_spec(tile, 1), _const_spec((1, RET_W)), _const_spec((1, RET_W)),
                  _const_spec((1, n_stack)), _const_spec((8, RET_W))],
        out_specs=_row_spec(tile, RET_W),
        out_shape=jax.ShapeDtypeStruct((n, RET_W), F32),
        scratch_shapes=scratch,
        compiler_params=_params(),
        name="retention",
    )(pc, pos_col, freq_full, lg_full, lg_stack, vec)


ROW_SUB = D_MODEL // 128


def _row_copy(src, src_row, dst, dst_row, sem):
    s0 = pl.multiple_of(src_row * ROW_SUB, ROW_SUB)
    d0 = pl.multiple_of(dst_row * ROW_SUB, ROW_SUB)
    return pltpu.make_async_copy(src.at[pl.ds(s0, ROW_SUB), :], dst.at[pl.ds(d0, ROW_SUB), :], sem)


def _to_row_tiles(ref, x):
    rows = x.shape[0]
    for s in range(ROW_SUB):
        ref[pl.ds(s, rows, stride=ROW_SUB), :] = x[:, s * 128:(s + 1) * 128]


def _from_row_tiles(ref, rows):
    return jnp.concatenate([ref[pl.ds(s, rows, stride=ROW_SUB), :] for s in range(ROW_SUB)], axis=1)


def _outproj_kernel(alpha, ya_ref, yb_ref, yc_ref, x_ref, w_ref, gt_ref, g_ref, b_ref, o_ref, ot_ref):
    mix = (_mm(ya_ref[...], w_ref[0:256, :]) + _mm(yb_ref[...], w_ref[256:768, :])
           + _mm(yc_ref[...], w_ref[768:1024, :]))
    x1 = _layer_norm_rows(alpha * x_ref[...] + (1.0 + gt_ref[...]) * mix, g_ref[...], b_ref[...])
    o_ref[...] = x1
    _to_row_tiles(ot_ref, x1)


def _outproj(alpha, ya, yb, yc, x2, w_out, gt, ln_g, ln_b):
    n, d = x2.shape
    tile = PROJ_TILE
    vecs = _const_spec((1, d))
    return pl.pallas_call(
        functools.partial(_outproj_kernel, alpha),
        grid=(n // tile,),
        in_specs=[_row_spec(tile, 256), _row_spec(tile, 512), _row_spec(tile, 256), _row_spec(tile, d),
                  _const_spec((d, d)), vecs, vecs, vecs],
        out_specs=[_row_spec(tile, d), _row_spec(tile * ROW_SUB, 128)],
        out_shape=[jax.ShapeDtypeStruct((n, d), F32), jax.ShapeDtypeStruct((n * ROW_SUB, 128), F32)],
        compiler_params=_params(),
        name="outproj_ln",
    )(ya, yb, yc, x2, w_out, gt, ln_g, ln_b)


def _router_kernel(x_ref, sc_ref, sh_ref, wt_ref, b_ref, idx_ref, rank_ref, gate_ref, cnt_ref, run_ref):
    @pl.when(pl.program_id(0) == 0)
    def _():
        run_ref[...] = jnp.zeros_like(run_ref)

    tile = x_ref.shape[0]
    h = x_ref[...] * (1.0 + sc_ref[...]) + sh_ref[...]
    logits = _mm3_nt(wt_ref[...], h) + b_ref[:, 0:1]
    e_iota = _iota2((N_EXPERTS, tile), 0).astype(F32)
    vals, hots, idxs = [], [], []
    for _ in range(TOP_K):
        m = jnp.max(logits, axis=0, keepdims=True)
        idx = jnp.min(jnp.where(logits == m, e_iota, float(N_EXPERTS)), axis=0, keepdims=True)
        hot = e_iota == idx
        logits = jnp.where(hot, -jnp.inf, logits)
        vals.append(m)
        idxs.append(idx.astype(I32))
        hots.append(hot)
    exps = [jnp.exp(vk - vals[0]) for vk in vals]
    denom = exps[0] + exps[1] + exps[2] + exps[3]
    hot_all = jnp.where(hots[0] | hots[1] | hots[2] | hots[3], 1.0, 0.0)
    before = jnp.where(_iota2((tile, tile), 0) < _iota2((tile, tile), 1), 1.0, 0.0).astype(BF16)
    seen = _dot(hot_all.astype(BF16), before) + run_ref[:, 0:1]
    for kk in range(TOP_K):
        idx_ref[kk:kk + 1, :] = idxs[kk]
        rank_ref[kk:kk + 1, :] = jnp.sum(jnp.where(hots[kk], seen, 0.0), axis=0, keepdims=True).astype(I32)
        gate_ref[kk:kk + 1, :] = exps[kk] / denom
    run_ref[...] = run_ref[...] + jnp.sum(hot_all, axis=1, keepdims=True)
    cnt_ref[...] = run_ref[...].astype(I32)


def _router(x1, sc, sh, router_wt, router_b):
    n, d = x1.shape
    tile = PROJ_TILE
    lane_spec = pl.BlockSpec((TOP_K, tile), lambda i: (0, i))
    return pl.pallas_call(
        _router_kernel,
        grid=(n // tile,),
        in_specs=[_row_spec(tile, d), _const_spec((1, d)), _const_spec((1, d)),
                  _const_spec((N_EXPERTS, d)), _const_spec((N_EXPERTS, 128))],
        out_specs=[lane_spec, lane_spec, lane_spec, _const_spec((N_EXPERTS, 128))],
        out_shape=[jax.ShapeDtypeStruct((TOP_K, n), I32), jax.ShapeDtypeStruct((TOP_K, n), I32),
                   jax.ShapeDtypeStruct((TOP_K, n), F32), jax.ShapeDtypeStruct((N_EXPERTS, 128), I32)],
        scratch_shapes=[pltpu.VMEM((N_EXPERTS, 128), F32)],
        compiler_params=_params(),
        name="router",
    )(x1, sc, sh, router_wt, router_b)


def _dest_kernel(pstart_ref, idx_ref, rank_ref, dest_ref):
    idx = idx_ref[...]
    dest = rank_ref[...]
    for e in range(N_EXPERTS):
        dest = dest + jnp.where(idx == e, pstart_ref[e], 0)
    dest_ref[...] = dest


def _dest(pstart, idx, rank):
    grid_spec = pltpu.PrefetchScalarGridSpec(
        num_scalar_prefetch=1, grid=(1,),
        in_specs=[pl.BlockSpec(idx.shape, lambda i, *_: (0, 0)), pl.BlockSpec(idx.shape, lambda i, *_: (0, 0))],
        out_specs=pl.BlockSpec(idx.shape, lambda i, *_: (0, 0)))
    return pl.pallas_call(_dest_kernel, grid_spec=grid_spec, out_shape=jax.ShapeDtypeStruct(idx.shape, I32),
                          compiler_params=_params(), name="moe_dest")(pstart, idx, rank)


def _rowmap_kernel(first, count, dest_ref, in_ref, out_ref, map_s, sem):
    load = pltpu.make_async_copy(in_ref, map_s, sem)
    load.start()
    load.wait()

    def scatter(i, c):
        for j in range(8):
            a = first + i * 8 + j
            map_s[dest_ref[a]] = a
        return c

    lax.fori_loop(0, count // 8, scatter, 0)
    store = pltpu.make_async_copy(map_s, out_ref, sem)
    store.start()
    store.wait()


def _rowmap(dest_flat, cap):
    n_assign = dest_flat.shape[0]
    rowmap = n_assign + jnp.arange(cap, dtype=I32) % MOE_ROWS
    halves = 2
    count = n_assign // halves
    for part in range(halves):
        grid_spec = pltpu.PrefetchScalarGridSpec(
            num_scalar_prefetch=1, grid=(1,), in_specs=[pl.BlockSpec(memory_space=pl.ANY)],
            out_specs=pl.BlockSpec(memory_space=pl.ANY),
            scratch_shapes=[pltpu.SMEM((cap,), I32), pltpu.SemaphoreType.DMA])
        rowmap = pl.pallas_call(functools.partial(_rowmap_kernel, part * count, count), grid_spec=grid_spec,
                                out_shape=jax.ShapeDtypeStruct((cap,), I32), compiler_params=_params(),
                                name="moe_rowmap")(dest_flat, rowmap)
    return rowmap


def _expert_kernel(n_tok, be_ref, map_ref, x_ref, sc_ref, sh_ref, wu_ref, bu_ref, wd_ref, bd_ref, out_ref,
                   xbuf, ybuf, wu_s, wd_s, gsem, ssem):
    b = pl.program_id(0)
    nb = pl.num_programs(0)
    slot = lax.rem(b, 2)
    other = 1 - slot
    n_assign = TOP_K * n_tok
    block_rows = MOE_ROWS * ROW_SUB

    def gather(block, j, into):
        m = map_ref[block * MOE_ROWS + j]
        tok = (m & (n_tok - 1)) if n_tok & (n_tok - 1) == 0 else lax.rem(m, n_tok)
        return _row_copy(x_ref, tok, xbuf.at[into], j, gsem.at[into])

    def scatter(block, j, frm, warm_up=None):
        m = map_ref[jnp.maximum(block, 0) * MOE_ROWS + j]
        dst = m if warm_up is None else jnp.where(warm_up, n_assign + j, m)
        return _row_copy(ybuf.at[frm], j, out_ref, dst, ssem.at[frm])

    def gather_wait(into):
        pltpu.make_async_copy(x_ref.at[pl.ds(0, block_rows), :], xbuf.at[into], gsem.at[into]).wait()

    def scatter_wait(frm):
        pltpu.make_async_copy(ybuf.at[frm], out_ref.at[pl.ds(0, block_rows), :], ssem.at[frm]).wait()

    def for_rows(fn):
        for j in range(MOE_ROWS):
            fn(j)

    @pl.when(b == 0)
    def _():
        ybuf[1] = jnp.zeros(ybuf.shape[1:], F32)
        for_rows(lambda j: gather(0, j, 0).start())

    gather_wait(slot)

    @pl.when(b >= 1)
    def _():
        scatter_wait(slot)

    @pl.when((b == 0) | (be_ref[b] != be_ref[jnp.maximum(b - 1, 0)]))
    def _():
        wu_s[...] = wu_ref[0, 0].astype(BF16)
        wd_s[...] = wd_ref[0, 0].astype(BF16)

    h = (_from_row_tiles(xbuf.at[slot], MOE_ROWS) * (1.0 + sc_ref[...]) + sh_ref[...]).astype(BF16)
    nxt = jnp.minimum(b + 1, nb - 1)
    for_rows(lambda j: gather(nxt, j, other).start())
    warm_up = b == 0
    for_rows(lambda j: scatter(b - 1, j, other, warm_up).start())
    up = _dot(h, wu_s[...]) + bu_ref[0, 0]
    glu = jnp.minimum(up[:, 0:D_FF], SWIGLU_LIMIT)
    lin = jnp.clip(up[:, D_FF:2 * D_FF], -SWIGLU_LIMIT, SWIGLU_LIMIT)
    act = glu * _sigmoid(SWIGLU_ALPHA * glu) * (lin + 1.0)
    _to_row_tiles(ybuf.at[slot], _dot(act.astype(BF16), wd_s[...]) + bd_ref[0, 0])

    @pl.when(b == nb - 1)
    def _():
        gather_wait(other)
        scatter_wait(other)
        for_rows(lambda j: scatter(b, j, slot).start())
        scatter_wait(slot)


def _experts(layer, block_e, rowmap, x1t, sc, sh, w_up, b_up, w_down, b_down):
    d = D_MODEL
    depth = w_up.shape[0]
    n_tok = x1t.shape[0] // ROW_SUB
    cap = rowmap.shape[0]
    n_blocks = cap // MOE_ROWS

    def wmap(b, be, mp):
        return (layer, be[b], 0, 0)

    vec = pl.BlockSpec((1, d), lambda b, be, mp: (0, 0))
    grid_spec = pltpu.PrefetchScalarGridSpec(
        num_scalar_prefetch=2,
        grid=(n_blocks,),
        in_specs=[pl.BlockSpec(memory_space=pl.ANY), vec, vec,
                  pl.BlockSpec((1, 1, d, 2 * D_FF), wmap), pl.BlockSpec((1, 1, 1, 2 * D_FF), wmap),
                  pl.BlockSpec((1, 1, D_FF, d), wmap), pl.BlockSpec((1, 1, 1, d), wmap)],
        out_specs=pl.BlockSpec(memory_space=pl.ANY),
        scratch_shapes=[pltpu.VMEM((2, MOE_ROWS * ROW_SUB, 128), F32), pltpu.VMEM((2, MOE_ROWS * ROW_SUB, 128), F32),
                        pltpu.VMEM((d, 2 * D_FF), BF16), pltpu.VMEM((D_FF, d), BF16),
                        pltpu.SemaphoreType.DMA((2,)), pltpu.SemaphoreType.DMA((2,))],
    )
    return pl.pallas_call(
        functools.partial(_expert_kernel, n_tok),
        grid_spec=grid_spec,
        out_shape=jax.ShapeDtypeStruct(((TOP_K * n_tok + MOE_ROWS) * ROW_SUB, 128), F32),
        compiler_params=_params(),
        name="moe_experts",
    )(block_e, rowmap, x1t, sc, sh, w_up, b_up.reshape(depth, N_EXPERTS, 1, 2 * D_FF), w_down,
      b_down.reshape(depth, N_EXPERTS, 1, d))


def _combine_kernel(alpha, gates_ref, y0_ref, y1_ref, y2_ref, y3_ref, x_ref, gt_ref, g_ref, b_ref, o_ref):
    tile = x_ref.shape[0]
    eye = jnp.where(_iota2((tile, tile), 0) == _iota2((tile, tile), 1), 1.0, 0.0).astype(BF16)
    gates = _sel_mm_nt(eye, jnp.concatenate([gates_ref[...], jnp.zeros((8 - TOP_K, tile), F32)], axis=0))
    ffn = _from_row_tiles(y0_ref, tile) * gates[:, 0:1]
    for kk, y_ref in enumerate((y1_ref, y2_ref, y3_ref), start=1):
        ffn = ffn + _from_row_tiles(y_ref, tile) * gates[:, kk:kk + 1]
    o_ref[...] = _layer_norm_rows(alpha * x_ref[...] + (1.0 + gt_ref[...]) * ffn, g_ref[...], b_ref[...])


def _combine(alpha, out4, gates, x1, gt, ln_g, ln_b):
    n, d = x1.shape
    tile = MOE_TILE
    steps = n // tile
    vec = _const_spec((1, d))
    y_specs = [pl.BlockSpec((tile * ROW_SUB, 128), functools.partial(lambda k, i: (k * steps + i, 0), k))
               for k in range(TOP_K)]
    return pl.pallas_call(
        functools.partial(_combine_kernel, alpha),
        grid=(steps,),
        in_specs=[pl.BlockSpec((TOP_K, tile), lambda i: (0, i))] + y_specs + [_row_spec(tile, d), vec, vec, vec],
        out_specs=_row_spec(tile, d),
        out_shape=jax.ShapeDtypeStruct((n, d), F32),
        compiler_params=_params(),
        name="moe_combine_ln",
    )(gates, out4, out4, out4, out4, x1, gt, ln_g, ln_b)


def _pad_rows(w, rows, offset=0):
    out = jnp.zeros((rows, w.shape[1]), w.dtype)
    return out.at[offset:offset + w.shape[0]].set(w)


def _moe(layer, alpha, x1, x1t, sc, sh, gt, router_w, router_b, w_up, b_up, w_down, b_down, ln_g, ln_b):
    n, d = x1.shape
    router_bias = jnp.broadcast_to(router_b[:, None], (N_EXPERTS, 128))
    idx, rank, gates, counts = _router(x1, sc, sh, router_w.T, router_bias)
    counts = counts[:, 0]
    padded = (counts + MOE_ROWS - 1) // MOE_ROWS * MOE_ROWS
    pend = jnp.cumsum(padded).astype(I32)
    pstart = pend - padded
    dest = _dest(pstart, idx, rank).reshape(TOP_K * n)
    n_blocks = -(-(n * TOP_K + N_EXPERTS * (MOE_ROWS - 1)) // MOE_ROWS)
    block_row = jnp.arange(n_blocks, dtype=I32)[:, None] * MOE_ROWS
    block_e = jnp.minimum(jnp.sum((pend[None, :] <= block_row).astype(I32), axis=1), N_EXPERTS - 1)
    rowmap = _rowmap(dest, n_blocks * MOE_ROWS)
    out4 = _experts(layer, block_e, rowmap, x1t, sc, sh, w_up, b_up, w_down, b_down)
    return _combine(alpha, out4, gates, x1, gt, ln_g, ln_b)


def kernel(x, c, positions, ada_w, ada_b, w_in, w_in_vres, tshift_mu, tshift_mu_vres, rwkv_w0, rwkv_w2, rwkv_a0, rwkv_a2, rwkv_g2, rwkv_kk, rwkv_ka, rwkv_rk, rwkv_ln_g, rwkv_ln_b, rwkv_v0, rwkv_v2, gdn_conv_w, gdn_a_log, gdn_dt_bias, gdn_norm_g, ret_norm_g, ret_norm_b, w_out, ln1_g, ln1_b, router_w, router_b, exp_w_up, exp_b_up, exp_w_down, exp_b_down, ln2_g, ln2_b):
    bsz, seq, d = x.shape
    assert bsz == 1 and d == D_MODEL and seq % MIX_TILE == 0
    depth = ada_w.shape[0]
    alpha = (2 * depth) ** 0.25
    n = seq
    x2 = x.reshape(n, d)
    ada = _ada(c, ada_w, ada_b)

    half = RET_HD // 2
    inv_freq = ROPE_BASE ** (-jnp.arange(half, dtype=F32) / half)
    freq_full = jnp.tile(inv_freq, 2 * RET_HEADS)[None, :]
    log_gamma = jnp.log1p(-jnp.exp2(-5.0 - jnp.arange(RET_HEADS, dtype=F32)))
    lg_full = jnp.repeat(log_gamma, RET_HD)[None, :]
    lg_stack = jnp.repeat(log_gamma, RET_CHUNK)[None, :]
    pos_col = positions.reshape(n, 1)

    v_first = None
    for l in range(depth):
        sh_mix, sc_mix, gt_mix, sh_ffn, sc_ffn, gt_ffn = (ada[l, :, i * d:(i + 1) * d] for i in range(6))
        w_l = w_in[l]
        zeros32 = jnp.zeros((d, 32), F32)
        vres_w = zeros32 if l == 0 else w_in_vres[l - 1]
        x_cols = jnp.concatenate([vres_w, w_l[:, 3072:3080], jnp.zeros((d, 88), F32)], axis=1)
        w_cat = jnp.concatenate([w_l[:, 0:1024], w_l[:, 1024:3072], w_l[:, 3080:4104], x_cols], axis=1).astype(BF16)
        mu_a = tshift_mu[l][None, :]
        mu_v = jnp.zeros((32,), F32) if l == 0 else tshift_mu_vres[l - 1]
        mu_x = jnp.concatenate([mu_v, jnp.zeros((96,), F32)])[None, :]
        pa, pb, pc, px = _inproj(x2, sc_mix, sh_mix, w_cat, mu_a, mu_x)

        v0 = jnp.zeros((RWKV_W,), F32) if l == 0 else rwkv_v0[l - 1]
        vec_a = jnp.stack([rwkv_w0[l], rwkv_a0[l], rwkv_kk[l], rwkv_ka[l], rwkv_rk[l].reshape(RWKV_W),
                           rwkv_ln_g[l], rwkv_ln_b[l], v0])
        w2p = _pad_rows(rwkv_w2[l], 128, 0).astype(BF16)
        a2p = _pad_rows(rwkv_a2[l], 128, 64).astype(BF16)
        g2 = rwkv_g2[l].astype(BF16)
        if l == 0:
            y_a, v_first = _rwkv(pa, None, None, vec_a, w2p, a2p, g2, None)
        else:
            v2p = _pad_rows(rwkv_v2[l - 1], 128, X_VRES).astype(BF16)
            y_a = _rwkv(pa, px, v_first, vec_a, w2p, a2p, g2, v2p)

        vec_b = jnp.concatenate([jnp.tile(gdn_norm_g[l], GDN_HEADS)[None, :], jnp.zeros((7, GDN_W), F32)])
        lane_pad = lambda t: jnp.concatenate([jnp.zeros((X_GA,), F32), t, jnp.zeros((128 - X_GA - GDN_HEADS,), F32)])
        vec_x = jnp.concatenate([jnp.stack([lane_pad(gdn_a_log[l]), lane_pad(gdn_dt_bias[l])]),
                                 jnp.zeros((6, 128), F32)])
        y_b = _gdn(pb, px, gdn_conv_w[l], vec_b, vec_x)

        vec_c = jnp.concatenate([jnp.stack([ret_norm_g[l], ret_norm_b[l]]), jnp.zeros((6, RET_W), F32)])
        y_c = _ret(pc, pos_col, freq_full, lg_full, lg_stack, vec_c)

        x1, x1t = _outproj(alpha, y_a, y_b, y_c, x2, w_out[l].astype(BF16), gt_mix, ln1_g[l][None, :],
                           ln1_b[l][None, :])
        x2 = _moe(l, alpha, x1, x1t, sc_ffn, sh_ffn, gt_ffn, router_w[l], router_b[l], exp_w_up, exp_b_up,
                  exp_w_down, exp_b_down, ln2_g[l][None, :], ln2_b[l][None, :])
    return x2.reshape(bsz, seq, d)
```

```python
import functools

import jax
import jax.numpy as jnp
from jax import lax
from jax.experimental import pallas as pl
from jax.experimental.pallas import tpu as pltpu

F32 = jnp.float32
BF16 = jnp.bfloat16
I32 = jnp.int32

D_MODEL = 1024
RWKV_HEADS, RWKV_HD, RWKV_W = 4, 64, 256
RWKV_GN_EPS = 64e-5
GDN_HEADS, GDN_HD, GDN_W = 4, 128, 512
GDN_CONV = 4
RET_HEADS, RET_HD, RET_W = 4, 64, 256
RET_CHUNK = 128
ROPE_BASE = 10000.0
N_EXPERTS, TOP_K, D_FF = 32, 4, 1024
SWIGLU_LIMIT, SWIGLU_ALPHA = 7.0, 1.702
LN_EPS, RMS_EPS = 1e-5, 1e-6

CHUNK = 64
STACK = 4 * CHUNK
MIX_TILE = 512
PROJ_TILE = 256
MOE_ROWS = 256
MOE_TILE = 256
NEG_BIG = -1e30
VMEM_LIMIT = 56 * 1024 * 1024

COL_A, COL_B, COL_C, COL_X = 0, 1024, 3072, 4096
N_PROJ = 4224
X_VRES, X_GA, X_GB = 0, 32, 36


def _dot(a, b):
    return jnp.dot(a, b, preferred_element_type=F32)


def _mm(a, b):
    return _dot(a.astype(BF16), b.astype(BF16))


def _mm_nt(a, b):
    return lax.dot_general(a.astype(BF16), b.astype(BF16), (((1,), (1,)), ((), ())),
                           preferred_element_type=F32)


def _split3(x):
    x1 = x.astype(BF16)
    r1 = x - x1.astype(F32)
    x2 = r1.astype(BF16)
    x3 = (r1 - x2.astype(F32)).astype(BF16)
    return x1, x2, x3


def _sel_mm(sel, x):
    x1, x2, x3 = _split3(x)
    return _dot(sel, x1) + (_dot(sel, x2) + _dot(sel, x3))


def _sel_mm_nt(sel, x):
    dn = (((1,), (1,)), ((), ()))
    x1, x2, x3 = _split3(x)
    d = lambda a, b: lax.dot_general(a, b, dn, preferred_element_type=F32)
    return d(sel, x1) + (d(sel, x2) + d(sel, x3))


def _mm3(a, b):
    a1 = a.astype(BF16)
    a2 = (a - a1.astype(F32)).astype(BF16)
    b1 = b.astype(BF16)
    b2 = (b - b1.astype(F32)).astype(BF16)
    return _dot(a1, b1) + (_dot(a1, b2) + _dot(a2, b1))


def _mm3_nt(a, b):
    dn = (((1,), (1,)), ((), ()))
    d = lambda x, y: lax.dot_general(x, y, dn, preferred_element_type=F32)
    a1 = a.astype(BF16)
    a2 = (a - a1.astype(F32)).astype(BF16)
    b1 = b.astype(BF16)
    b2 = (b - b1.astype(F32)).astype(BF16)
    return d(a1, b1) + (d(a1, b2) + d(a2, b1))


def _iota2(shape, axis):
    return lax.broadcasted_iota(I32, shape, axis)


def _softplus(x):
    return jnp.maximum(x, 0.0) + jnp.log1p(jnp.exp(-jnp.abs(x)))


def _sigmoid(x):
    return jax.nn.sigmoid(x)


def _silu(x):
    return x * jax.nn.sigmoid(x)


def _same_block(n_rows, n_cols, block):
    r = _iota2((n_rows, n_cols), 0)
    c = _iota2((n_rows, n_cols), 1)
    return (r // block) == (c // block), r, c


def _block_cumsum_mat(n, block):
    same, r, c = _same_block(n, n, block)
    return jnp.where(same & (c <= r), 1.0, 0.0).astype(BF16)


def _head_masks(width, head_dim):
    lane = _iota2((1, width), 1)
    return [jnp.where((lane // head_dim) == h, 1.0, 0.0).astype(F32) for h in range(width // head_dim)]


def _stack_heads(x, masks):
    return jnp.concatenate([x * m for m in masks], axis=0)


def _fold_heads(x4, n_heads):
    c = x4.shape[0] // n_heads
    out = x4[0:c]
    for h in range(1, n_heads):
        out = out + x4[h * c:(h + 1) * c]
    return out


def _chunk_last_row(x, chunk):
    rows, width = x.shape
    x3 = x.reshape(rows // chunk, chunk, width)
    return jnp.broadcast_to(x3[:, chunk - 1:chunk, :], x3.shape).reshape(rows, width)


def _unit_lower_inverses(ms):
    n = ms[0].shape[0]
    eye = jnp.where(_iota2((n, n), 0) == _iota2((n, n), 1), 1.0, 0.0).astype(F32)
    ts = [eye - m for m in ms]
    ps = [m.astype(BF16) for m in ms]
    for step in range(5):
        ps = [_dot(p, p).astype(BF16) for p in ps]
        ts = [t + _dot(t.astype(BF16), p) for t, p in zip(ts, ps)]
    return ts


def _layer_norm_rows(x, g, b):
    mu = jnp.mean(x, axis=-1, keepdims=True)
    xc = x - mu
    var = jnp.mean(xc * xc, axis=-1, keepdims=True)
    return xc * lax.rsqrt(var + LN_EPS) * g + b


def _shift_down(x, prev8, d):
    rolled = pltpu.roll(x, d, axis=0)
    head = pltpu.roll(jnp.concatenate([prev8, x[0:8]], axis=0), d, axis=0)[8:16]
    return jnp.concatenate([head, rolled[8:]], axis=0)


def _head_sums(x, head_dim):
    lane = _iota2((1, 128), 1)
    parts = []
    for j in range(x.shape[1] // 128):
        slab = x[:, j * 128:(j + 1) * 128]
        if head_dim == 128:
            parts.append(jnp.broadcast_to(jnp.sum(slab, axis=-1, keepdims=True), slab.shape))
        else:
            lo = lane < 64
            s_lo = jnp.sum(jnp.where(lo, slab, 0.0), axis=-1, keepdims=True)
            s_hi = jnp.sum(jnp.where(lo, 0.0, slab), axis=-1, keepdims=True)
            parts.append(jnp.where(lo, s_lo, s_hi))
    return jnp.concatenate(parts, axis=1)


def _params(n_grid_axes=1, vmem=VMEM_LIMIT):
    return pltpu.CompilerParams(dimension_semantics=("arbitrary",) * n_grid_axes, vmem_limit_bytes=vmem)


def _row_spec(tile, width):
    return pl.BlockSpec((tile, width), lambda i: (i, 0))


def _const_spec(shape):
    zeros = (0,) * len(shape)
    return pl.BlockSpec(shape, lambda i: zeros)


def _ada_kernel(c_ref, w_ref, b_ref, o_ref):
    c = c_ref[...]
    o_ref[0] = _mm3(_silu(c), w_ref[0]) + b_ref[0]


def _ada(c, ada_w, ada_b):
    depth, d, d6 = ada_w.shape
    blk = 1024
    c8 = jnp.broadcast_to(c, (8, d))
    out = pl.pallas_call(
        _ada_kernel,
        grid=(depth, d6 // blk),
        in_specs=[pl.BlockSpec((8, d), lambda l, j: (0, 0)),
                  pl.BlockSpec((1, d, blk), lambda l, j: (l, 0, j)),
                  pl.BlockSpec((1, 1, blk), lambda l, j: (l, 0, j))],
        out_specs=pl.BlockSpec((1, 8, blk), lambda l, j: (l, 0, j)),
        out_shape=jax.ShapeDtypeStruct((depth, 8, d6), F32),
        compiler_params=_params(2),
        name="ada",
    )(c8, ada_w, ada_b.reshape(depth, 1, d6))
    return out[:, 0:1, :]


def _inproj_kernel(x_ref, sc_ref, sh_ref, w_ref, mua_ref, mux_ref, pa_ref, pb_ref, pc_ref, px_ref, carry_ref):
    @pl.when(pl.program_id(0) == 0)
    def _():
        carry_ref[...] = jnp.zeros_like(carry_ref)

    tile = x_ref.shape[0]
    h = (x_ref[...] * (1.0 + sc_ref[...]) + sh_ref[...]).astype(BF16)
    pa = _dot(h, w_ref[:, COL_A:COL_B])
    px = _dot(h, w_ref[:, COL_X:N_PROJ])
    pb_ref[...] = _dot(h, w_ref[:, COL_B:COL_C])
    pc_ref[...] = _dot(h, w_ref[:, COL_C:COL_X])
    prev_a = _shift_down(pa, carry_ref[:, 0:1024], 1)
    prev_x = _shift_down(px, carry_ref[:, 1024:1152], 1)
    carry_ref[:, 0:1024] = pa[tile - 8:tile, :]
    carry_ref[:, 1024:1152] = px[tile - 8:tile, :]
    pa_ref[...] = pa + (prev_a - pa) * mua_ref[...]
    px_ref[...] = px + (prev_x - px) * mux_ref[...]


def _inproj(x2, sc, sh, w_cat, mu_a, mu_x):
    n, d = x2.shape
    tile = PROJ_TILE
    return pl.pallas_call(
        _inproj_kernel,
        grid=(n // tile,),
        in_specs=[_row_spec(tile, d), _const_spec((1, d)), _const_spec((1, d)),
                  _const_spec((d, N_PROJ)), _const_spec((1, 1024)), _const_spec((1, 128))],
        out_specs=[_row_spec(tile, 1024), _row_spec(tile, 2048), _row_spec(tile, 1024), _row_spec(tile, 128)],
        out_shape=[jax.ShapeDtypeStruct((n, 1024), F32), jax.ShapeDtypeStruct((n, 2048), F32),
                   jax.ShapeDtypeStruct((n, 1024), F32), jax.ShapeDtypeStruct((n, 128), F32)],
        scratch_shapes=[pltpu.VMEM((8, 1152), F32)],
        compiler_params=_params(),
        name="inproj",
    )(x2, sc, sh, w_cat, mu_a, mu_x)


def _rwkv_kernel(has_vres, *refs):
    if has_vres:
        (pa_ref, px_ref, vf_ref, vec_ref, w2_ref, a2_ref, g2_ref, v2_ref,
         y_ref, state_ref, rt_s, at_s, kt_s, bt_s, kh_s, bh_s, v_s, gc_s, y_s) = refs
    else:
        (pa_ref, vec_ref, w2_ref, a2_ref, g2_ref,
         y_ref, vout_ref, state_ref, rt_s, at_s, kt_s, bt_s, kh_s, bh_s, v_s, gc_s, y_s) = refs

    @pl.when(pl.program_id(0) == 0)
    def _():
        state_ref[...] = jnp.zeros_like(state_ref)

    tile = pa_ref.shape[0]
    w0, a0, k_k, k_a, r_k, ln_g, ln_b, v0 = (vec_ref[i:i + 1, :] for i in range(8))
    r = pa_ref[:, 0:256]
    k = pa_ref[:, 256:512]
    v = pa_ref[:, 512:768]
    wa = pa_ref[:, 768:896]
    g_lo = pa_ref[:, 896:1024]

    w_raw = -_softplus(-(w0 + _mm(jnp.tanh(wa), w2_ref[...]))) - 0.5
    log_decay = -jnp.exp(w_raw)
    a = _sigmoid(a0 + _mm(wa, a2_ref[...]))
    gate = _mm(_sigmoid(g_lo), g2_ref[...])
    if has_vres:
        v = v + (vf_ref[...] - v) * _sigmoid(v0 + _mm(px_ref[...], v2_ref[...]))
    else:
        vout_ref[...] = v

    kk = k * k_k
    kk = kk * lax.rsqrt(_head_sums(kk * kk, RWKV_HD) + 1e-12)
    k = k * (1.0 + (a - 1.0) * k_a)
    bonus = _head_sums(r * k * r_k, RWKV_HD) * v

    cl = _sel_mm(_block_cumsum_mat(tile, CHUNK), log_decay)
    cl_end = _chunk_last_row(cl, CHUNK)
    e_inv = jnp.exp(-cl)
    e_end = jnp.exp(cl_end - cl)
    kka = kk * a
    rt_s[...] = r * jnp.exp(cl)
    at_s[...] = -kk * jnp.exp(cl - log_decay)
    kt_s[...] = k * e_inv
    bt_s[...] = kka * e_inv
    kh_s[...] = k * e_end
    bh_s[...] = kka * e_end
    v_s[...] = v
    gc_s[...] = jnp.exp(cl_end)

    masks = _head_masks(RWKV_W, RWKV_HD)
    same, row, col = _same_block(STACK, STACK, CHUNK)
    strict = same & (row > col)
    incl = same & (row >= col)
    same_head = (_iota2((RWKV_W, RWKV_W), 0) // RWKV_HD) == (_iota2((RWKV_W, RWKV_W), 1) // RWKV_HD)

    chunks = range(tile // CHUNK)
    rows = [slice(c * CHUNK, (c + 1) * CHUNK) for c in chunks]
    at4 = [_stack_heads(at_s[r, :], masks).astype(BF16) for r in rows]
    rt4 = [_stack_heads(rt_s[r, :], masks).astype(BF16) for r in rows]
    kb4 = [jnp.concatenate([_stack_heads(kt_s[r, :], masks), _stack_heads(bt_s[r, :], masks)], axis=0).astype(BF16)
           for r in rows]
    v4 = [_stack_heads(v_s[r, :], masks).astype(BF16) for r in rows]
    s_a = [_mm_nt(at4[c], kb4[c]) for c in chunks]
    s_r = [_mm_nt(rt4[c], kb4[c]) for c in chunks]
    a_ak = [jnp.where(strict, s_a[c][:, 0:STACK], 0.0).astype(BF16) for c in chunks]
    a_rk = [jnp.where(incl, s_r[c][:, 0:STACK], 0.0).astype(BF16) for c in chunks]
    a_rb = [jnp.where(incl, s_r[c][:, STACK:2 * STACK], 0.0).astype(BF16) for c in chunks]
    t_inv = _unit_lower_inverses([jnp.where(strict, -s_a[c][:, STACK:2 * STACK], 0.0) for c in chunks])
    w4 = [_mm(t_inv[c], at4[c]).astype(BF16) for c in chunks]
    akv = [_mm(a_ak[c], v4[c]) for c in chunks]
    u0 = [_mm(t_inv[c], akv[c]) for c in chunks]
    yv = [_mm(a_rk[c], v4[c]) for c in chunks]

    state = state_ref[...]
    for c in chunks:
        r = rows[c]
        u4 = _mm_nt(w4[c], state) + u0[c]
        y4 = yv[c] + _mm(a_rb[c], u4)
        y_s[r, :] = _mm_nt(rt_s[r, :], state) + _fold_heads(y4, RWKV_HEADS)
        zt = jnp.concatenate([v_s[r, :], _fold_heads(u4, RWKV_HEADS)], axis=0).T
        kb = jnp.concatenate([kh_s[r, :], bh_s[r, :]], axis=0)
        state = state * gc_s[c * CHUNK:c * CHUNK + 1, :] + jnp.where(same_head, _mm(zt, kb), 0.0)
    state_ref[...] = state

    y = y_s[...]
    inv_n = 1.0 / RWKV_HD
    mu = _head_sums(y, RWKV_HD) * inv_n
    yc = y - mu
    var = _head_sums(yc * yc, RWKV_HD) * inv_n
    yn = yc * lax.rsqrt(var + RWKV_GN_EPS) * ln_g + ln_b
    y_ref[...] = (yn + bonus) * gate


def _rwkv(pa, px, v_first, vec, w2p, a2p, g2, v2p):
    n = pa.shape[0]
    tile = MIX_TILE
    has_vres = v_first is not None
    big = [pltpu.VMEM((tile, RWKV_W), F32) for _ in range(9)]
    scratch = [pltpu.VMEM((RWKV_W, RWKV_W), F32)] + big
    wspec = _const_spec((128, RWKV_W))
    if has_vres:
        in_specs = [_row_spec(tile, 1024), _row_spec(tile, 128), _row_spec(tile, RWKV_W),
                    _const_spec((8, RWKV_W)), wspec, wspec, wspec, wspec]
        args = (pa, px, v_first, vec, w2p, a2p, g2, v2p)
        out_specs = _row_spec(tile, RWKV_W)
        out_shape = jax.ShapeDtypeStruct((n, RWKV_W), F32)
    else:
        in_specs = [_row_spec(tile, 1024), _const_spec((8, RWKV_W)), wspec, wspec, wspec]
        args = (pa, vec, w2p, a2p, g2)
        out_specs = [_row_spec(tile, RWKV_W), _row_spec(tile, RWKV_W)]
        out_shape = [jax.ShapeDtypeStruct((n, RWKV_W), F32), jax.ShapeDtypeStruct((n, RWKV_W), F32)]
    return pl.pallas_call(
        functools.partial(_rwkv_kernel, has_vres),
        grid=(n // tile,),
        in_specs=in_specs, out_specs=out_specs, out_shape=out_shape,
        scratch_shapes=scratch,
        compiler_params=_params(),
        name="rwkv7_vres" if has_vres else "rwkv7",
    )(*args)


def _gdn_kernel(pb_ref, px_ref, convw_ref, vec_ref, vecx_ref, y_ref,
                state_ref, carry_ref, q_s, qd_s, k_s, kb_s, kd_s, rhs_s, gcs_s, gl_s, o_s):
    @pl.when(pl.program_id(0) == 0)
    def _():
        state_ref[...] = jnp.zeros_like(state_ref)
        carry_ref[...] = jnp.zeros_like(carry_ref)

    tile = pb_ref.shape[0]
    width3 = 3 * GDN_W
    norm_g = vec_ref[0:1, :]

    x = pb_ref[:, 0:width3]
    prev8 = carry_ref[...]
    acc = x * convw_ref[GDN_CONV - 1:GDN_CONV, :]
    for d in range(1, GDN_CONV):
        acc = acc + _shift_down(x, prev8, d) * convw_ref[GDN_CONV - 1 - d:GDN_CONV - d, :]
    carry_ref[...] = x[tile - 8:tile, :]
    qkv = _silu(acc)
    q = qkv[:, 0:GDN_W]
    k = qkv[:, GDN_W:2 * GDN_W]
    v = qkv[:, 2 * GDN_W:width3]
    q = q * lax.rsqrt(_head_sums(q * q, GDN_HD) + 1e-12) * (GDN_HD ** -0.5)
    k = k * lax.rsqrt(_head_sums(k * k, GDN_HD) + 1e-12)

    px = px_ref[...]
    g_x = -jnp.exp(vecx_ref[0:1, :]) * _softplus(px + vecx_ref[1:2, :])
    beta_x = _sigmoid(px)
    gc_x = _sel_mm(_block_cumsum_mat(tile, CHUNK), g_x)
    gl_x = _chunk_last_row(gc_x, CHUNK)
    eg_x = jnp.exp(gc_x)
    ed_x = jnp.exp(gl_x - gc_x)
    el_x = jnp.exp(gl_x)
    for h in range(GDN_HEADS):
        hs = slice(h * GDN_HD, (h + 1) * GDN_HD)
        col = lambda t, lane0: t[:, lane0 + h:lane0 + h + 1]
        beta, eg = col(beta_x, X_GB), col(eg_x, X_GA)
        kb = k[:, hs] * beta
        q_s[:, hs] = q[:, hs]
        qd_s[:, hs] = q[:, hs] * eg
        k_s[:, hs] = k[:, hs]
        kb_s[:, hs] = kb
        kd_s[:, hs] = k[:, hs] * col(ed_x, X_GA)
        rhs_s[:, hs] = v[:, hs] * beta
        rhs_s[:, GDN_W + h * GDN_HD:GDN_W + (h + 1) * GDN_HD] = kb * eg
        gcs_s[:, hs] = jnp.broadcast_to(col(gc_x, X_GA), (tile, GDN_HD))
        gl_s[:, hs] = jnp.broadcast_to(col(el_x, X_GA), (tile, GDN_HD))

    same, rr, cc = _same_block(STACK, STACK, CHUNK)
    strict = same & (rr > cc)
    incl = same & (rr >= cc)
    lane = _iota2((STACK, GDN_HD), 1)
    pick3 = jnp.where(lane < 3, 1.0, 0.0).astype(BF16)
    row_head = [jnp.where((_iota2((1, STACK), 1) // CHUNK) == h, 1.0, 0.0).astype(F32) for h in range(GDN_HEADS)]

    def stack(x):
        return jnp.concatenate([x[:, h * GDN_HD:(h + 1) * GDN_HD] for h in range(GDN_HEADS)], axis=0)

    def unstack(y):
        return jnp.concatenate([y[h * CHUNK:(h + 1) * CHUNK] for h in range(GDN_HEADS)], axis=1)

    def per_head(fn):
        return [fn(h, slice(h * CHUNK, (h + 1) * CHUNK)) for h in range(GDN_HEADS)]

    def decay_mask(r):
        gcol = stack(gcs_s[r, :])
        col_form = jnp.concatenate([gcol, gcol], axis=1)
        g1, g2, g3 = _split3(gcol)
        pieces = jnp.where(lane == 0, g1, jnp.where(lane == 1, g2, jnp.where(lane == 2, g3, jnp.zeros_like(g1))))
        row_form = lax.dot_general(pick3, pieces, (((1,), (1,)), ((), ())), preferred_element_type=F32)
        return jnp.exp(jnp.where(incl, col_form - row_form, NEG_BIG))

    chunks = range(tile // CHUNK)
    rows = [slice(c * CHUNK, (c + 1) * CHUNK) for c in chunks]
    k_c = [stack(k_s[r, :]).astype(BF16) for r in rows]
    s_kb = [_mm_nt(stack(kb_s[r, :]), k_c[c]) for c, r in enumerate(rows)]
    s_q = [_mm_nt(stack(q_s[r, :]), k_c[c]) for c, r in enumerate(rows)]
    decay = [decay_mask(r) for r in rows]
    attn = [jnp.where(incl, s_q[c] * decay[c], 0.0).astype(BF16) for c in chunks]
    t_inv = _unit_lower_inverses([jnp.where(strict, s_kb[c] * decay[c], 0.0) for c in chunks])
    sol = [_mm(t_inv[c], jnp.concatenate([stack(rhs_s[r, 0:GDN_W]), stack(rhs_s[r, GDN_W:2 * GDN_W])], axis=1))
           for c, r in enumerate(rows)]

    states = [state_ref[h] for h in range(GDN_HEADS)]
    for c in chunks:
        r = rows[c]
        u_c, w_c = sol[c][:, 0:GDN_HD], sol[c][:, GDN_HD:2 * GDN_HD]
        qd_c = stack(qd_s[r, :])
        kd_t = stack(kd_s[r, :]).T
        v_new = u_c - jnp.concatenate(per_head(lambda h, hr: _mm(w_c[hr], states[h])), axis=0)
        o_c = jnp.concatenate(per_head(lambda h, hr: _mm(qd_c[hr], states[h])), axis=0) + _mm(attn[c], v_new)
        o_s[r, :] = unstack(o_c)
        gl = gl_s[c * CHUNK:c * CHUNK + 1, :]
        states = [states[h] * gl[:, h * GDN_HD:(h + 1) * GDN_HD] + _mm(kd_t * row_head[h], v_new)
                  for h in range(GDN_HEADS)]
    for h in range(GDN_HEADS):
        state_ref[h] = states[h]

    o = o_s[...]
    ms = _head_sums(o * o, GDN_HD) * (1.0 / GDN_HD)
    z = pb_ref[:, width3:width3 + GDN_W]
    y_ref[...] = o * lax.rsqrt(ms + RMS_EPS) * norm_g * _silu(z)


def _gdn(pb, px, conv_w, vec, vec_x):
    n = pb.shape[0]
    tile = MIX_TILE
    wide = lambda w: pltpu.VMEM((tile, w), F32)
    scratch = [pltpu.VMEM((GDN_HEADS, GDN_HD, GDN_HD), F32), pltpu.VMEM((8, 3 * GDN_W), F32),
               wide(GDN_W), wide(GDN_W), wide(GDN_W), wide(GDN_W), wide(GDN_W), wide(2 * GDN_W),
               wide(GDN_W), wide(GDN_W), wide(GDN_W)]
    return pl.pallas_call(
        _gdn_kernel,
        grid=(n // tile,),
        in_specs=[_row_spec(tile, 2048), _row_spec(tile, 128), _const_spec((GDN_CONV, 3 * GDN_W)),
                  _const_spec((8, GDN_W)), _const_spec((8, 128))],
        out_specs=_row_spec(tile, GDN_W),
        out_shape=jax.ShapeDtypeStruct((n, GDN_W), F32),
        scratch_shapes=scratch,
        compiler_params=_params(),
        name="gdn",
    )(pb, px, conv_w, vec, vec_x)


def _ret_kernel(pc_ref, pos_ref, freq_ref, lg_ref, lgs_ref, vec_ref, y_ref,
                state_ref, dmat_ref, xi_ref, zeta_ref, q_s, k_s, v_s, o_s):
    n_stack = RET_HEADS * RET_CHUNK

    @pl.when(pl.program_id(0) == 0)
    def _():
        state_ref[...] = jnp.zeros_like(state_ref)
        same, r, c = _same_block(n_stack, n_stack, RET_CHUNK)
        dist = (r - c).astype(F32)
        dmat_ref[...] = jnp.exp(jnp.where(same & (r >= c), dist * lgs_ref[...], NEG_BIG))
        idx = (_iota2((n_stack, RET_W), 0) % RET_CHUNK).astype(F32)
        xi_ref[...] = jnp.exp((idx + 1.0) * lg_ref[...])
        zeta_ref[...] = jnp.exp((RET_CHUNK - 1.0 - idx) * lg_ref[...])

    tile = pc_ref.shape[0]
    ln_g, ln_b = vec_ref[0:1, :], vec_ref[1:2, :]
    first_half = (_iota2((1, 128), 1) % RET_HD) < RET_HD // 2
    ang = pos_ref[...].astype(F32) * freq_ref[:, 0:128]
    cos_slab = jnp.cos(ang)
    sin_slab = jnp.where(first_half, -1.0, 1.0) * jnp.sin(ang)
    cos = jnp.concatenate([cos_slab] * (RET_W // 128), axis=1)
    sin_signed = jnp.concatenate([sin_slab] * (RET_W // 128), axis=1)

    def partner(t):
        slabs = [t[:, j * 128:(j + 1) * 128] for j in range(RET_W // 128)]
        return jnp.concatenate([jnp.where(first_half, pltpu.roll(sl, 128 - RET_HD // 2, axis=1),
                                          pltpu.roll(sl, RET_HD // 2, axis=1)) for sl in slabs], axis=1)

    def rotary(t):
        return t * cos + partner(t) * sin_signed

    q_s[...] = rotary(pc_ref[:, 0:RET_W])
    k_s[...] = rotary(pc_ref[:, RET_W:2 * RET_W]) * (RET_HD ** -0.5)
    v_s[...] = pc_ref[:, 2 * RET_W:3 * RET_W]
    gate = pc_ref[:, 3 * RET_W:4 * RET_W]
    masks = _head_masks(RET_W, RET_HD)
    chunk_decay = jnp.exp(float(RET_CHUNK) * lg_ref[...])

    def chunk_body(c, carry):
        rows = pl.ds(pl.multiple_of(c * RET_CHUNK, RET_CHUNK), RET_CHUNK)
        q4 = _stack_heads(q_s[rows, :], masks)
        k4 = _stack_heads(k_s[rows, :], masks)
        v4 = _stack_heads(v_s[rows, :], masks)
        scores = _mm_nt(q4, k4) * dmat_ref[...]
        state = state_ref[...]
        o4 = _mm(scores, v4) + _mm(q4 * xi_ref[...], state)
        o_s[rows, :] = _fold_heads(o4, RET_HEADS)
        state_ref[...] = state * chunk_decay + _mm((k4 * zeta_ref[...]).T, v4)
        return carry

    lax.fori_loop(0, tile // RET_CHUNK, chunk_body, 0)

    o = o_s[...]
    inv_n = 1.0 / RET_HD
    mu = _head_sums(o, RET_HD) * inv_n
    oc = o - mu
    var = _head_sums(oc * oc, RET_HD) * inv_n
    y_ref[...] = (oc * lax.rsqrt(var + LN_EPS) * ln_g + ln_b) * _silu(gate)


def _ret(pc, pos_col, freq_full, lg_full, lg_stack, vec):
    n = pc.shape[0]
    tile = MIX_TILE
    n_stack = RET_HEADS * RET_CHUNK
    scratch = [pltpu.VMEM((RET_W, RET_W), F32), pltpu.VMEM((n_stack, n_stack), F32),
               pltpu.VMEM((n_stack, RET_W), F32), pltpu.VMEM((n_stack, RET_W), F32)]
    scratch += [pltpu.VMEM((tile, RET_W), F32) for _ in range(4)]
    return pl.pallas_call(
        _ret_kernel,
        grid=(n // tile,),
        in_specs=[_row_spec(tile, 1024), _row_spec(tile, 1), _const_spec((1, RET_W)), _const_spec((1, RET_W)),
                  _const_spec((1, n_stack)), _const_spec((8, RET_W))],
        out_specs=_row_spec(tile, RET_W),
        out_shape=jax.ShapeDtypeStruct((n, RET_W), F32),
        scratch_shapes=scratch,
        compiler_params=_params(),
        name="retention",
    )(pc, pos_col, freq_full, lg_full, lg_stack, vec)


ROW_SUB = D_MODEL // 128


def _row_copy(src, src_row, dst, dst_row, sem):
    s0 = pl.multiple_of(src_row * ROW_SUB, ROW_SUB)
    d0 = pl.multiple_of(dst_row * ROW_SUB, ROW_SUB)
    return pltpu.make_async_copy(src.at[pl.ds(s0, ROW_SUB), :], dst.at[pl.ds(d0, ROW_SUB), :], sem)


def _to_row_tiles(ref, x):
    rows = x.shape[0]
    for s in range(ROW_SUB):
        ref[pl.ds(s, rows, stride=ROW_SUB), :] = x[:, s * 128:(s + 1) * 128]


def _from_row_tiles(ref, rows):
    return jnp.concatenate([ref[pl.ds(s, rows, stride=ROW_SUB), :] for s in range(ROW_SUB)], axis=1)


def _outproj_kernel(alpha, ya_ref, yb_ref, yc_ref, x_ref, w_ref, gt_ref, g_ref, b_ref, o_ref, ot_ref):
    mix = (_mm(ya_ref[...], w_ref[0:256, :]) + _mm(yb_ref[...], w_ref[256:768, :])
           + _mm(yc_ref[...], w_ref[768:1024, :]))
    x1 = _layer_norm_rows(alpha * x_ref[...] + (1.0 + gt_ref[...]) * mix, g_ref[...], b_ref[...])
    o_ref[...] = x1
    _to_row_tiles(ot_ref, x1)


def _outproj(alpha, ya, yb, yc, x2, w_out, gt, ln_g, ln_b):
    n, d = x2.shape
    tile = PROJ_TILE
    vecs = _const_spec((1, d))
    return pl.pallas_call(
        functools.partial(_outproj_kernel, alpha),
        grid=(n // tile,),
        in_specs=[_row_spec(tile, 256), _row_spec(tile, 512), _row_spec(tile, 256), _row_spec(tile, d),
                  _const_spec((d, d)), vecs, vecs, vecs],
        out_specs=[_row_spec(tile, d), _row_spec(tile * ROW_SUB, 128)],
        out_shape=[jax.ShapeDtypeStruct((n, d), F32), jax.ShapeDtypeStruct((n * ROW_SUB, 128), F32)],
        compiler_params=_params(),
        name="outproj_ln",
    )(ya, yb, yc, x2, w_out, gt, ln_g, ln_b)


def _router_kernel(x_ref, sc_ref, sh_ref, wt_ref, b_ref, idx_ref, rank_ref, gate_ref, cnt_ref, run_ref):
    @pl.when(pl.program_id(0) == 0)
    def _():
        run_ref[...] = jnp.zeros_like(run_ref)

    tile = x_ref.shape[0]
    h = x_ref[...] * (1.0 + sc_ref[...]) + sh_ref[...]
    logits = _mm3_nt(wt_ref[...], h) + b_ref[:, 0:1]
    e_iota = _iota2((N_EXPERTS, tile), 0).astype(F32)
    vals, hots, idxs = [], [], []
    for _ in range(TOP_K):
        m = jnp.max(logits, axis=0, keepdims=True)
        idx = jnp.min(jnp.where(logits == m, e_iota, float(N_EXPERTS)), axis=0, keepdims=True)
        hot = e_iota == idx
        logits = jnp.where(hot, -jnp.inf, logits)
        vals.append(m)
        idxs.append(idx.astype(I32))
        hots.append(hot)
    exps = [jnp.exp(vk - vals[0]) for vk in vals]
    denom = exps[0] + exps[1] + exps[2] + exps[3]
    hot_all = jnp.where(hots[0] | hots[1] | hots[2] | hots[3], 1.0, 0.0)
    before = jnp.where(_iota2((tile, tile), 0) < _iota2((tile, tile), 1), 1.0, 0.0).astype(BF16)
    seen = _dot(hot_all.astype(BF16), before) + run_ref[:, 0:1]
    for kk in range(TOP_K):
        idx_ref[kk:kk + 1, :] = idxs[kk]
        rank_ref[kk:kk + 1, :] = jnp.sum(jnp.where(hots[kk], seen, 0.0), axis=0, keepdims=True).astype(I32)
        gate_ref[kk:kk + 1, :] = exps[kk] / denom
    run_ref[...] = run_ref[...] + jnp.sum(hot_all, axis=1, keepdims=True)
    cnt_ref[...] = run_ref[...].astype(I32)


def _router(x1, sc, sh, router_wt, router_b):
    n, d = x1.shape
    tile = PROJ_TILE
    lane_spec = pl.BlockSpec((TOP_K, tile), lambda i: (0, i))
    return pl.pallas_call(
        _router_kernel,
        grid=(n // tile,),
        in_specs=[_row_spec(tile, d), _const_spec((1, d)), _const_spec((1, d)),
                  _const_spec((N_EXPERTS, d)), _const_spec((N_EXPERTS, 128))],
        out_specs=[lane_spec, lane_spec, lane_spec, _const_spec((N_EXPERTS, 128))],
        out_shape=[jax.ShapeDtypeStruct((TOP_K, n), I32), jax.ShapeDtypeStruct((TOP_K, n), I32),
                   jax.ShapeDtypeStruct((TOP_K, n), F32), jax.ShapeDtypeStruct((N_EXPERTS, 128), I32)],
        scratch_shapes=[pltpu.VMEM((N_EXPERTS, 128), F32)],
        compiler_params=_params(),
        name="router",
    )(x1, sc, sh, router_wt, router_b)


def _dest_kernel(pstart_ref, idx_ref, rank_ref, dest_ref):
    idx = idx_ref[...]
    dest = rank_ref[...]
    for e in range(N_EXPERTS):
        dest = dest + jnp.where(idx == e, pstart_ref[e], 0)
    dest_ref[...] = dest


def _dest(pstart, idx, rank):
    grid_spec = pltpu.PrefetchScalarGridSpec(
        num_scalar_prefetch=1, grid=(1,),
        in_specs=[pl.BlockSpec(idx.shape, lambda i, *_: (0, 0)), pl.BlockSpec(idx.shape, lambda i, *_: (0, 0))],
        out_specs=pl.BlockSpec(idx.shape, lambda i, *_: (0, 0)))
    return pl.pallas_call(_dest_kernel, grid_spec=grid_spec, out_shape=jax.ShapeDtypeStruct(idx.shape, I32),
                          compiler_params=_params(), name="moe_dest")(pstart, idx, rank)


def _rowmap_kernel(first, count, dest_ref, in_ref, out_ref, map_s, sem):
    load = pltpu.make_async_copy(in_ref, map_s, sem)
    load.start()
    load.wait()

    unroll = 32

    def scatter(i, c):
        rows = [dest_ref[first + i * unroll + j] for j in range(unroll)]
        for j in range(unroll):
            map_s[rows[j]] = first + i * unroll + j
        return c

    lax.fori_loop(0, count // unroll, scatter, 0)
    store = pltpu.make_async_copy(map_s, out_ref, sem)
    store.start()
    store.wait()


def _rowmap(dest_flat, cap):
    n_assign = dest_flat.shape[0]
    rowmap = n_assign + jnp.arange(cap, dtype=I32) % MOE_ROWS
    halves = 2
    count = n_assign // halves
    for part in range(halves):
        grid_spec = pltpu.PrefetchScalarGridSpec(
            num_scalar_prefetch=1, grid=(1,), in_specs=[pl.BlockSpec(memory_space=pl.ANY)],
            out_specs=pl.BlockSpec(memory_space=pl.ANY),
            scratch_shapes=[pltpu.SMEM((cap,), I32), pltpu.SemaphoreType.DMA])
        rowmap = pl.pallas_call(functools.partial(_rowmap_kernel, part * count, count), grid_spec=grid_spec,
                                out_shape=jax.ShapeDtypeStruct((cap,), I32), compiler_params=_params(),
                                name="moe_rowmap")(dest_flat, rowmap)
    return rowmap


def _expert_kernel(n_tok, be_ref, map_ref, x_ref, sc_ref, sh_ref, wu_ref, bu_ref, wd_ref, bd_ref, out_ref,
                   xbuf, ybuf, wu_s, wd_s, gsem, ssem):
    b = pl.program_id(0)
    nb = pl.num_programs(0)
    slot = lax.rem(b, 2)
    other = 1 - slot
    n_assign = TOP_K * n_tok
    block_rows = MOE_ROWS * ROW_SUB

    def gather(block, j, into):
        m = map_ref[block * MOE_ROWS + j]
        tok = (m & (n_tok - 1)) if n_tok & (n_tok - 1) == 0 else lax.rem(m, n_tok)
        return _row_copy(x_ref, tok, xbuf.at[into], j, gsem.at[into])

    def scatter(block, j, frm, warm_up=None):
        m = map_ref[jnp.maximum(block, 0) * MOE_ROWS + j]
        dst = m if warm_up is None else jnp.where(warm_up, n_assign + j, m)
        return _row_copy(ybuf.at[frm], j, out_ref, dst, ssem.at[frm])

    def gather_wait(into):
        pltpu.make_async_copy(x_ref.at[pl.ds(0, block_rows), :], xbuf.at[into], gsem.at[into]).wait()

    def scatter_wait(frm):
        pltpu.make_async_copy(ybuf.at[frm], out_ref.at[pl.ds(0, block_rows), :], ssem.at[frm]).wait()

    def for_rows(fn):
        for j in range(MOE_ROWS):
            fn(j)

    @pl.when(b == 0)
    def _():
        ybuf[1] = jnp.zeros(ybuf.shape[1:], F32)
        for_rows(lambda j: gather(0, j, 0).start(priority=j % 2))

    gather_wait(slot)

    @pl.when(b >= 1)
    def _():
        scatter_wait(slot)

    @pl.when((b == 0) | (be_ref[b] != be_ref[jnp.maximum(b - 1, 0)]))
    def _():
        wu_s[...] = wu_ref[0, 0].astype(BF16)
        wd_s[...] = wd_ref[0, 0].astype(BF16)

    h = (_from_row_tiles(xbuf.at[slot], MOE_ROWS) * (1.0 + sc_ref[...]) + sh_ref[...]).astype(BF16)
    nxt = jnp.minimum(b + 1, nb - 1)
    for_rows(lambda j: gather(nxt, j, other).start(priority=j % 2))
    warm_up = b == 0
    for_rows(lambda j: scatter(b - 1, j, other, warm_up).start(priority=j % 2))
    up = _dot(h, wu_s[...]) + bu_ref[0, 0]
    glu = jnp.minimum(up[:, 0:D_FF], SWIGLU_LIMIT)
    lin = jnp.clip(up[:, D_FF:2 * D_FF], -SWIGLU_LIMIT, SWIGLU_LIMIT)
    act = glu * _sigmoid(SWIGLU_ALPHA * glu) * (lin + 1.0)
    _to_row_tiles(ybuf.at[slot], _dot(act.astype(BF16), wd_s[...]) + bd_ref[0, 0])

    @pl.when(b == nb - 1)
    def _():
        gather_wait(other)
        scatter_wait(other)
        for_rows(lambda j: scatter(b, j, slot).start(priority=j % 2))
        scatter_wait(slot)


def _experts(layer, block_e, rowmap, x1t, sc, sh, w_up, b_up, w_down, b_down):
    d = D_MODEL
    depth = w_up.shape[0]
    n_tok = x1t.shape[0] // ROW_SUB
    cap = rowmap.shape[0]
    n_blocks = cap // MOE_ROWS

    def wmap(b, be, mp):
        return (layer, be[b], 0, 0)

    vec = pl.BlockSpec((1, d), lambda b, be, mp: (0, 0))
    grid_spec = pltpu.PrefetchScalarGridSpec(
        num_scalar_prefetch=2,
        grid=(n_blocks,),
        in_specs=[pl.BlockSpec(memory_space=pl.ANY), vec, vec,
                  pl.BlockSpec((1, 1, d, 2 * D_FF), wmap), pl.BlockSpec((1, 1, 1, 2 * D_FF), wmap),
                  pl.BlockSpec((1, 1, D_FF, d), wmap), pl.BlockSpec((1, 1, 1, d), wmap)],
        out_specs=pl.BlockSpec(memory_space=pl.ANY),
        scratch_shapes=[pltpu.VMEM((2, MOE_ROWS * ROW_SUB, 128), F32), pltpu.VMEM((2, MOE_ROWS * ROW_SUB, 128), F32),
                        pltpu.VMEM((d, 2 * D_FF), BF16), pltpu.VMEM((D_FF, d), BF16),
                        pltpu.SemaphoreType.DMA((2,)), pltpu.SemaphoreType.DMA((2,))],
    )
    return pl.pallas_call(
        functools.partial(_expert_kernel, n_tok),
        grid_spec=grid_spec,
        out_shape=jax.ShapeDtypeStruct(((TOP_K * n_tok + MOE_ROWS) * ROW_SUB, 128), F32),
        compiler_params=_params(),
        name="moe_experts",
    )(block_e, rowmap, x1t, sc, sh, w_up, b_up.reshape(depth, N_EXPERTS, 1, 2 * D_FF), w_down,
      b_down.reshape(depth, N_EXPERTS, 1, d))


def _combine_kernel(alpha, gates_ref, y0_ref, y1_ref, y2_ref, y3_ref, x_ref, gt_ref, g_ref, b_ref, o_ref):
    tile = x_ref.shape[0]
    eye = jnp.where(_iota2((tile, tile), 0) == _iota2((tile, tile), 1), 1.0, 0.0).astype(BF16)
    gates = _sel_mm_nt(eye, jnp.concatenate([gates_ref[...], jnp.zeros((8 - TOP_K, tile), F32)], axis=0))
    ffn = _from_row_tiles(y0_ref, tile) * gates[:, 0:1]
    for kk, y_ref in enumerate((y1_ref, y2_ref, y3_ref), start=1):
        ffn = ffn + _from_row_tiles(y_ref, tile) * gates[:, kk:kk + 1]
    o_ref[...] = _layer_norm_rows(alpha * x_ref[...] + (1.0 + gt_ref[...]) * ffn, g_ref[...], b_ref[...])


def _combine(alpha, out4, gates, x1, gt, ln_g, ln_b):
    n, d = x1.shape
    tile = MOE_TILE
    steps = n // tile
    vec = _const_spec((1, d))
    y_specs = [pl.BlockSpec((tile * ROW_SUB, 128), functools.partial(lambda k, i: (k * steps + i, 0), k))
               for k in range(TOP_K)]
    return pl.pallas_call(
        functools.partial(_combine_kernel, alpha),
        grid=(steps,),
        in_specs=[pl.BlockSpec((TOP_K, tile), lambda i: (0, i))] + y_specs + [_row_spec(tile, d), vec, vec, vec],
        out_specs=_row_spec(tile, d),
        out_shape=jax.ShapeDtypeStruct((n, d), F32),
        compiler_params=_params(),
        name="moe_combine_ln",
    )(gates, out4, out4, out4, out4, x1, gt, ln_g, ln_b)


def _pad_rows(w, rows, offset=0):
    out = jnp.zeros((rows, w.shape[1]), w.dtype)
    return out.at[offset:offset + w.shape[0]].set(w)


def _moe(layer, alpha, x1, x1t, sc, sh, gt, router_w, router_b, w_up, b_up, w_down, b_down, ln_g, ln_b):
    n, d = x1.shape
    router_bias = jnp.broadcast_to(router_b[:, None], (N_EXPERTS, 128))
    idx, rank, gates, counts = _router(x1, sc, sh, router_w.T, router_bias)
    counts = counts[:, 0]
    padded = (counts + MOE_ROWS - 1) // MOE_ROWS * MOE_ROWS
    pend = jnp.cumsum(padded).astype(I32)
    pstart = pend - padded
    dest = _dest(pstart, idx, rank).reshape(TOP_K * n)
    n_blocks = -(-(n * TOP_K + N_EXPERTS * (MOE_ROWS - 1)) // MOE_ROWS)
    block_row = jnp.arange(n_blocks, dtype=I32)[:, None] * MOE_ROWS
    block_e = jnp.minimum(jnp.sum((pend[None, :] <= block_row).astype(I32), axis=1), N_EXPERTS - 1)
    rowmap = _rowmap(dest, n_blocks * MOE_ROWS)
    out4 = _experts(layer, block_e, rowmap, x1t, sc, sh, w_up, b_up, w_down, b_down)
    return _combine(alpha, out4, gates, x1, gt, ln_g, ln_b)


def kernel(x, c, positions, ada_w, ada_b, w_in, w_in_vres, tshift_mu, tshift_mu_vres, rwkv_w0, rwkv_w2, rwkv_a0, rwkv_a2, rwkv_g2, rwkv_kk, rwkv_ka, rwkv_rk, rwkv_ln_g, rwkv_ln_b, rwkv_v0, rwkv_v2, gdn_conv_w, gdn_a_log, gdn_dt_bias, gdn_norm_g, ret_norm_g, ret_norm_b, w_out, ln1_g, ln1_b, router_w, router_b, exp_w_up, exp_b_up, exp_w_down, exp_b_down, ln2_g, ln2_b):
    bsz, seq, d = x.shape
    assert bsz == 1 and d == D_MODEL and seq % MIX_TILE == 0
    depth = ada_w.shape[0]
    alpha = (2 * depth) ** 0.25
    n = seq
    x2 = x.reshape(n, d)
    ada = _ada(c, ada_w, ada_b)

    half = RET_HD // 2
    inv_freq = ROPE_BASE ** (-jnp.arange(half, dtype=F32) / half)
    freq_full = jnp.tile(inv_freq, 2 * RET_HEADS)[None, :]
    log_gamma = jnp.log1p(-jnp.exp2(-5.0 - jnp.arange(RET_HEADS, dtype=F32)))
    lg_full = jnp.repeat(log_gamma, RET_HD)[None, :]
    lg_stack = jnp.repeat(log_gamma, RET_CHUNK)[None, :]
    pos_col = positions.reshape(n, 1)

    v_first = None
    for l in range(depth):
        sh_mix, sc_mix, gt_mix, sh_ffn, sc_ffn, gt_ffn = (ada[l, :, i * d:(i + 1) * d] for i in range(6))
        w_l = w_in[l]
        zeros32 = jnp.zeros((d, 32), F32)
        vres_w = zeros32 if l == 0 else w_in_vres[l - 1]
        x_cols = jnp.concatenate([vres_w, w_l[:, 3072:3080], jnp.zeros((d, 88), F32)], axis=1)
        w_cat = jnp.concatenate([w_l[:, 0:1024], w_l[:, 1024:3072], w_l[:, 3080:4104], x_cols], axis=1).astype(BF16)
        mu_a = tshift_mu[l][None, :]
        mu_v = jnp.zeros((32,), F32) if l == 0 else tshift_mu_vres[l - 1]
        mu_x = jnp.concatenate([mu_v, jnp.zeros((96,), F32)])[None, :]
        pa, pb, pc, px = _inproj(x2, sc_mix, sh_mix, w_cat, mu_a, mu_x)

        v0 = jnp.zeros((RWKV_W,), F32) if l == 0 else rwkv_v0[l - 1]
        vec_a = jnp.stack([rwkv_w0[l], rwkv_a0[l], rwkv_kk[l], rwkv_ka[l], rwkv_rk[l].reshape(RWKV_W),
                           rwkv_ln_g[l], rwkv_ln_b[l], v0])
        w2p = _pad_rows(rwkv_w2[l], 128, 0).astype(BF16)
        a2p = _pad_rows(rwkv_a2[l], 128, 64).astype(BF16)
        g2 = rwkv_g2[l].astype(BF16)
        if l == 0:
            y_a, v_first = _rwkv(pa, None, None, vec_a, w2p, a2p, g2, None)
        else:
            v2p = _pad_rows(rwkv_v2[l - 1], 128, X_VRES).astype(BF16)
            y_a = _rwkv(pa, px, v_first, vec_a, w2p, a2p, g2, v2p)

        vec_b = jnp.concatenate([jnp.tile(gdn_norm_g[l], GDN_HEADS)[None, :], jnp.zeros((7, GDN_W), F32)])
        lane_pad = lambda t: jnp.concatenate([jnp.zeros((X_GA,), F32), t, jnp.zeros((128 - X_GA - GDN_HEADS,), F32)])
        vec_x = jnp.concatenate([jnp.stack([lane_pad(gdn_a_log[l]), lane_pad(gdn_dt_bias[l])]),
                                 jnp.zeros((6, 128), F32)])
        y_b = _gdn(pb, px, gdn_conv_w[l], vec_b, vec_x)

        vec_c = jnp.concatenate([jnp.stack([ret_norm_g[l], ret_norm_b[l]]), jnp.zeros((6, RET_W), F32)])
        y_c = _ret(pc, pos_col, freq_full, lg_full, lg_stack, vec_c)

        x1, x1t = _outproj(alpha, y_a, y_b, y_c, x2, w_out[l].astype(BF16), gt_mix, ln1_g[l][None, :],
                           ln1_b[l][None, :])
        x2 = _moe(l, alpha, x1, x1t, sc_ffn, sh_ffn, gt_ffn, router_w[l], router_b[l], exp_w_up, exp_b_up,
                  exp_w_down, exp_b_down, ln2_g[l][None, :], ln2_b[l][None, :])
    return x2.reshape(bsz, seq, d)
```

```python
import functools

import jax
import jax.numpy as jnp
from jax import lax
from jax.experimental import pallas as pl
from jax.experimental.pallas import tpu as pltpu

F32 = jnp.float32
BF16 = jnp.bfloat16
I32 = jnp.int32

D_MODEL = 1024
RWKV_HEADS, RWKV_HD, RWKV_W = 4, 64, 256
RWKV_GN_EPS = 64e-5
GDN_HEADS, GDN_HD, GDN_W = 4, 128, 512
GDN_CONV = 4
RET_HEADS, RET_HD, RET_W = 4, 64, 256
RET_CHUNK = 128
ROPE_BASE = 10000.0
N_EXPERTS, TOP_K, D_FF = 32, 4, 1024
SWIGLU_LIMIT, SWIGLU_ALPHA = 7.0, 1.702
LN_EPS, RMS_EPS = 1e-5, 1e-6

CHUNK = 64
STACK = 4 * CHUNK
MIX_TILE = 512
PROJ_TILE = 256
MOE_ROWS = 256
MOE_TILE = 256
MOE_RING = 3
NEG_BIG = -1e30
VMEM_LIMIT = 56 * 1024 * 1024

COL_A, COL_B, COL_C, COL_X = 0, 1024, 3072, 4096
N_PROJ = 4224
X_VRES, X_GA, X_GB = 0, 32, 36


def _dot(a, b):
    return jnp.dot(a, b, preferred_element_type=F32)


def _mm(a, b):
    return _dot(a.astype(BF16), b.astype(BF16))


def _mm_nt(a, b):
    return lax.dot_general(a.astype(BF16), b.astype(BF16), (((1,), (1,)), ((), ())),
                           preferred_element_type=F32)


def _split3(x):
    x1 = x.astype(BF16)
    r1 = x - x1.astype(F32)
    x2 = r1.astype(BF16)
    x3 = (r1 - x2.astype(F32)).astype(BF16)
    return x1, x2, x3


def _sel_mm(sel, x):
    x1, x2, x3 = _split3(x)
    return _dot(sel, x1) + (_dot(sel, x2) + _dot(sel, x3))


def _sel_mm_nt(sel, x):
    dn = (((1,), (1,)), ((), ()))
    x1, x2, x3 = _split3(x)
    d = lambda a, b: lax.dot_general(a, b, dn, preferred_element_type=F32)
    return d(sel, x1) + (d(sel, x2) + d(sel, x3))


def _mm3(a, b):
    a1 = a.astype(BF16)
    a2 = (a - a1.astype(F32)).astype(BF16)
    b1 = b.astype(BF16)
    b2 = (b - b1.astype(F32)).astype(BF16)
    return _dot(a1, b1) + (_dot(a1, b2) + _dot(a2, b1))


def _mm3_nt(a, b):
    dn = (((1,), (1,)), ((), ()))
    d = lambda x, y: lax.dot_general(x, y, dn, preferred_element_type=F32)
    a1 = a.astype(BF16)
    a2 = (a - a1.astype(F32)).astype(BF16)
    b1 = b.astype(BF16)
    b2 = (b - b1.astype(F32)).astype(BF16)
    return d(a1, b1) + (d(a1, b2) + d(a2, b1))


def _iota2(shape, axis):
    return lax.broadcasted_iota(I32, shape, axis)


def _softplus(x):
    return jnp.maximum(x, 0.0) + jnp.log1p(jnp.exp(-jnp.abs(x)))


def _sigmoid(x):
    return jax.nn.sigmoid(x)


def _silu(x):
    return x * jax.nn.sigmoid(x)


def _same_block(n_rows, n_cols, block):
    r = _iota2((n_rows, n_cols), 0)
    c = _iota2((n_rows, n_cols), 1)
    return (r // block) == (c // block), r, c


def _block_cumsum_mat(n, block):
    same, r, c = _same_block(n, n, block)
    return jnp.where(same & (c <= r), 1.0, 0.0).astype(BF16)


def _head_masks(width, head_dim):
    lane = _iota2((1, width), 1)
    return [jnp.where((lane // head_dim) == h, 1.0, 0.0).astype(F32) for h in range(width // head_dim)]


def _stack_heads(x, masks):
    return jnp.concatenate([x * m for m in masks], axis=0)


def _fold_heads(x4, n_heads):
    c = x4.shape[0] // n_heads
    out = x4[0:c]
    for h in range(1, n_heads):
        out = out + x4[h * c:(h + 1) * c]
    return out


def _chunk_last_row(x, chunk):
    rows, width = x.shape
    x3 = x.reshape(rows // chunk, chunk, width)
    return jnp.broadcast_to(x3[:, chunk - 1:chunk, :], x3.shape).reshape(rows, width)


def _unit_lower_inverses(ms):
    n = ms[0].shape[0]
    eye = jnp.where(_iota2((n, n), 0) == _iota2((n, n), 1), 1.0, 0.0).astype(F32)
    ts = [eye - m for m in ms]
    ps = [m.astype(BF16) for m in ms]
    for step in range(5):
        ps = [_dot(p, p).astype(BF16) for p in ps]
        ts = [t + _dot(t.astype(BF16), p) for t, p in zip(ts, ps)]
    return ts


def _layer_norm_rows(x, g, b):
    mu = jnp.mean(x, axis=-1, keepdims=True)
    xc = x - mu
    var = jnp.mean(xc * xc, axis=-1, keepdims=True)
    return xc * lax.rsqrt(var + LN_EPS) * g + b


def _shift_down(x, prev8, d):
    rolled = pltpu.roll(x, d, axis=0)
    head = pltpu.roll(jnp.concatenate([prev8, x[0:8]], axis=0), d, axis=0)[8:16]
    return jnp.concatenate([head, rolled[8:]], axis=0)


def _head_sums(x, head_dim):
    lane = _iota2((1, 128), 1)
    parts = []
    for j in range(x.shape[1] // 128):
        slab = x[:, j * 128:(j + 1) * 128]
        if head_dim == 128:
            parts.append(jnp.broadcast_to(jnp.sum(slab, axis=-1, keepdims=True), slab.shape))
        else:
            lo = lane < 64
            s_lo = jnp.sum(jnp.where(lo, slab, 0.0), axis=-1, keepdims=True)
            s_hi = jnp.sum(jnp.where(lo, 0.0, slab), axis=-1, keepdims=True)
            parts.append(jnp.where(lo, s_lo, s_hi))
    return jnp.concatenate(parts, axis=1)


def _params(n_grid_axes=1, vmem=VMEM_LIMIT):
    return pltpu.CompilerParams(dimension_semantics=("arbitrary",) * n_grid_axes, vmem_limit_bytes=vmem)


def _row_spec(tile, width):
    return pl.BlockSpec((tile, width), lambda i: (i, 0))


def _const_spec(shape):
    zeros = (0,) * len(shape)
    return pl.BlockSpec(shape, lambda i: zeros)


def _ada_kernel(c_ref, w_ref, b_ref, o_ref):
    c = c_ref[...]
    o_ref[0] = _mm3(_silu(c), w_ref[0]) + b_ref[0]


def _ada(c, ada_w, ada_b):
    depth, d, d6 = ada_w.shape
    blk = 1024
    c8 = jnp.broadcast_to(c, (8, d))
    out = pl.pallas_call(
        _ada_kernel,
        grid=(depth, d6 // blk),
        in_specs=[pl.BlockSpec((8, d), lambda l, j: (0, 0)),
                  pl.BlockSpec((1, d, blk), lambda l, j: (l, 0, j)),
                  pl.BlockSpec((1, 1, blk), lambda l, j: (l, 0, j))],
        out_specs=pl.BlockSpec((1, 8, blk), lambda l, j: (l, 0, j)),
        out_shape=jax.ShapeDtypeStruct((depth, 8, d6), F32),
        compiler_params=_params(2),
        name="ada",
    )(c8, ada_w, ada_b.reshape(depth, 1, d6))
    return out[:, 0:1, :]


def _inproj_kernel(x_ref, sc_ref, sh_ref, w_ref, mua_ref, mux_ref, pa_ref, pb_ref, pc_ref, px_ref, carry_ref):
    @pl.when(pl.program_id(0) == 0)
    def _():
        carry_ref[...] = jnp.zeros_like(carry_ref)

    tile = x_ref.shape[0]
    h = (x_ref[...] * (1.0 + sc_ref[...]) + sh_ref[...]).astype(BF16)
    pa = _dot(h, w_ref[:, COL_A:COL_B])
    px = _dot(h, w_ref[:, COL_X:N_PROJ])
    pb_ref[...] = _dot(h, w_ref[:, COL_B:COL_C])
    pc_ref[...] = _dot(h, w_ref[:, COL_C:COL_X])
    prev_a = _shift_down(pa, carry_ref[:, 0:1024], 1)
    prev_x = _shift_down(px, carry_ref[:, 1024:1152], 1)
    carry_ref[:, 0:1024] = pa[tile - 8:tile, :]
    carry_ref[:, 1024:1152] = px[tile - 8:tile, :]
    pa_ref[...] = pa + (prev_a - pa) * mua_ref[...]
    px_ref[...] = px + (prev_x - px) * mux_ref[...]


def _inproj(x2, sc, sh, w_cat, mu_a, mu_x):
    n, d = x2.shape
    tile = PROJ_TILE
    return pl.pallas_call(
        _inproj_kernel,
        grid=(n // tile,),
        in_specs=[_row_spec(tile, d), _const_spec((1, d)), _const_spec((1, d)),
                  _const_spec((d, N_PROJ)), _const_spec((1, 1024)), _const_spec((1, 128))],
        out_specs=[_row_spec(tile, 1024), _row_spec(tile, 2048), _row_spec(tile, 1024), _row_spec(tile, 128)],
        out_shape=[jax.ShapeDtypeStruct((n, 1024), F32), jax.ShapeDtypeStruct((n, 2048), F32),
                   jax.ShapeDtypeStruct((n, 1024), F32), jax.ShapeDtypeStruct((n, 128), F32)],
        scratch_shapes=[pltpu.VMEM((8, 1152), F32)],
        compiler_params=_params(),
        name="inproj",
    )(x2, sc, sh, w_cat, mu_a, mu_x)


def _rwkv_kernel(has_vres, *refs):
    if has_vres:
        (pa_ref, px_ref, vf_ref, vec_ref, w2_ref, a2_ref, g2_ref, v2_ref,
         y_ref, state_ref, rt_s, at_s, kt_s, bt_s, kh_s, bh_s, v_s, gc_s, y_s) = refs
    else:
        (pa_ref, vec_ref, w2_ref, a2_ref, g2_ref,
         y_ref, vout_ref, state_ref, rt_s, at_s, kt_s, bt_s, kh_s, bh_s, v_s, gc_s, y_s) = refs

    @pl.when(pl.program_id(0) == 0)
    def _():
        state_ref[...] = jnp.zeros_like(state_ref)

    tile = pa_ref.shape[0]
    w0, a0, k_k, k_a, r_k, ln_g, ln_b, v0 = (vec_ref[i:i + 1, :] for i in range(8))
    r = pa_ref[:, 0:256]
    k = pa_ref[:, 256:512]
    v = pa_ref[:, 512:768]
    wa = pa_ref[:, 768:896]
    g_lo = pa_ref[:, 896:1024]

    w_raw = -_softplus(-(w0 + _mm(jnp.tanh(wa), w2_ref[...]))) - 0.5
    log_decay = -jnp.exp(w_raw)
    a = _sigmoid(a0 + _mm(wa, a2_ref[...]))
    gate = _mm(_sigmoid(g_lo), g2_ref[...])
    if has_vres:
        v = v + (vf_ref[...] - v) * _sigmoid(v0 + _mm(px_ref[...], v2_ref[...]))
    else:
        vout_ref[...] = v

    kk = k * k_k
    kk = kk * lax.rsqrt(_head_sums(kk * kk, RWKV_HD) + 1e-12)
    k = k * (1.0 + (a - 1.0) * k_a)
    bonus = _head_sums(r * k * r_k, RWKV_HD) * v

    cl = _sel_mm(_block_cumsum_mat(tile, CHUNK), log_decay)
    cl_end = _chunk_last_row(cl, CHUNK)
    e_inv = jnp.exp(-cl)
    e_end = jnp.exp(cl_end - cl)
    kka = kk * a
    rt_s[...] = r * jnp.exp(cl)
    at_s[...] = -kk * jnp.exp(cl - log_decay)
    kt_s[...] = k * e_inv
    bt_s[...] = kka * e_inv
    kh_s[...] = k * e_end
    bh_s[...] = kka * e_end
    v_s[...] = v
    gc_s[...] = jnp.exp(cl_end)

    masks = _head_masks(RWKV_W, RWKV_HD)
    same, row, col = _same_block(STACK, STACK, CHUNK)
    strict = same & (row > col)
    incl = same & (row >= col)
    same_head = (_iota2((RWKV_W, RWKV_W), 0) // RWKV_HD) == (_iota2((RWKV_W, RWKV_W), 1) // RWKV_HD)

    chunks = range(tile // CHUNK)
    rows = [slice(c * CHUNK, (c + 1) * CHUNK) for c in chunks]
    at4 = [_stack_heads(at_s[r, :], masks).astype(BF16) for r in rows]
    rt4 = [_stack_heads(rt_s[r, :], masks).astype(BF16) for r in rows]
    kb4 = [jnp.concatenate([_stack_heads(kt_s[r, :], masks), _stack_heads(bt_s[r, :], masks)], axis=0).astype(BF16)
           for r in rows]
    v4 = [_stack_heads(v_s[r, :], masks).astype(BF16) for r in rows]
    s_a = [_mm_nt(at4[c], kb4[c]) for c in chunks]
    s_r = [_mm_nt(rt4[c], kb4[c]) for c in chunks]
    a_ak = [jnp.where(strict, s_a[c][:, 0:STACK], 0.0).astype(BF16) for c in chunks]
    a_rk = [jnp.where(incl, s_r[c][:, 0:STACK], 0.0).astype(BF16) for c in chunks]
    a_rb = [jnp.where(incl, s_r[c][:, STACK:2 * STACK], 0.0).astype(BF16) for c in chunks]
    t_inv = _unit_lower_inverses([jnp.where(strict, -s_a[c][:, STACK:2 * STACK], 0.0) for c in chunks])
    w4 = [_mm(t_inv[c], at4[c]).astype(BF16) for c in chunks]
    akv = [_mm(a_ak[c], v4[c]) for c in chunks]
    u0 = [_mm(t_inv[c], akv[c]) for c in chunks]
    yv = [_mm(a_rk[c], v4[c]) for c in chunks]

    state = state_ref[...]
    for c in chunks:
        r = rows[c]
        u4 = _mm_nt(w4[c], state) + u0[c]
        y4 = yv[c] + _mm(a_rb[c], u4)
        y_s[r, :] = _mm_nt(rt_s[r, :], state) + _fold_heads(y4, RWKV_HEADS)
        zt = jnp.concatenate([v_s[r, :], _fold_heads(u4, RWKV_HEADS)], axis=0).T
        kb = jnp.concatenate([kh_s[r, :], bh_s[r, :]], axis=0)
        state = state * gc_s[c * CHUNK:c * CHUNK + 1, :] + jnp.where(same_head, _mm(zt, kb), 0.0)
    state_ref[...] = state

    y = y_s[...]
    inv_n = 1.0 / RWKV_HD
    mu = _head_sums(y, RWKV_HD) * inv_n
    yc = y - mu
    var = _head_sums(yc * yc, RWKV_HD) * inv_n
    yn = yc * lax.rsqrt(var + RWKV_GN_EPS) * ln_g + ln_b
    y_ref[...] = (yn + bonus) * gate


def _rwkv(pa, px, v_first, vec, w2p, a2p, g2, v2p):
    n = pa.shape[0]
    tile = MIX_TILE
    has_vres = v_first is not None
    big = [pltpu.VMEM((tile, RWKV_W), F32) for _ in range(9)]
    scratch = [pltpu.VMEM((RWKV_W, RWKV_W), F32)] + big
    wspec = _const_spec((128, RWKV_W))
    if has_vres:
        in_specs = [_row_spec(tile, 1024), _row_spec(tile, 128), _row_spec(tile, RWKV_W),
                    _const_spec((8, RWKV_W)), wspec, wspec, wspec, wspec]
        args = (pa, px, v_first, vec, w2p, a2p, g2, v2p)
        out_specs = _row_spec(tile, RWKV_W)
        out_shape = jax.ShapeDtypeStruct((n, RWKV_W), F32)
    else:
        in_specs = [_row_spec(tile, 1024), _const_spec((8, RWKV_W)), wspec, wspec, wspec]
        args = (pa, vec, w2p, a2p, g2)
        out_specs = [_row_spec(tile, RWKV_W), _row_spec(tile, RWKV_W)]
        out_shape = [jax.ShapeDtypeStruct((n, RWKV_W), F32), jax.ShapeDtypeStruct((n, RWKV_W), F32)]
    return pl.pallas_call(
        functools.partial(_rwkv_kernel, has_vres),
        grid=(n // tile,),
        in_specs=in_specs, out_specs=out_specs, out_shape=out_shape,
        scratch_shapes=scratch,
        compiler_params=_params(),
        name="rwkv7_vres" if has_vres else "rwkv7",
    )(*args)


def _gdn_kernel(pb_ref, px_ref, convw_ref, vec_ref, vecx_ref, y_ref,
                state_ref, carry_ref, q_s, qd_s, k_s, kb_s, kd_s, rhs_s, gcs_s, gl_s, o_s):
    @pl.when(pl.program_id(0) == 0)
    def _():
        state_ref[...] = jnp.zeros_like(state_ref)
        carry_ref[...] = jnp.zeros_like(carry_ref)

    tile = pb_ref.shape[0]
    width3 = 3 * GDN_W
    norm_g = vec_ref[0:1, :]

    x = pb_ref[:, 0:width3]
    prev8 = carry_ref[...]
    acc = x * convw_ref[GDN_CONV - 1:GDN_CONV, :]
    for d in range(1, GDN_CONV):
        acc = acc + _shift_down(x, prev8, d) * convw_ref[GDN_CONV - 1 - d:GDN_CONV - d, :]
    carry_ref[...] = x[tile - 8:tile, :]
    qkv = _silu(acc)
    q = qkv[:, 0:GDN_W]
    k = qkv[:, GDN_W:2 * GDN_W]
    v = qkv[:, 2 * GDN_W:width3]
    q = q * lax.rsqrt(_head_sums(q * q, GDN_HD) + 1e-12) * (GDN_HD ** -0.5)
    k = k * lax.rsqrt(_head_sums(k * k, GDN_HD) + 1e-12)

    px = px_ref[...]
    g_x = -jnp.exp(vecx_ref[0:1, :]) * _softplus(px + vecx_ref[1:2, :])
    beta_x = _sigmoid(px)
    gc_x = _sel_mm(_block_cumsum_mat(tile, CHUNK), g_x)
    gl_x = _chunk_last_row(gc_x, CHUNK)
    eg_x = jnp.exp(gc_x)
    ed_x = jnp.exp(gl_x - gc_x)
    el_x = jnp.exp(gl_x)
    for h in range(GDN_HEADS):
        hs = slice(h * GDN_HD, (h + 1) * GDN_HD)
        col = lambda t, lane0: t[:, lane0 + h:lane0 + h + 1]
        beta, eg = col(beta_x, X_GB), col(eg_x, X_GA)
        kb = k[:, hs] * beta
        q_s[:, hs] = q[:, hs]
        qd_s[:, hs] = q[:, hs] * eg
        k_s[:, hs] = k[:, hs]
        kb_s[:, hs] = kb
        kd_s[:, hs] = k[:, hs] * col(ed_x, X_GA)
        rhs_s[:, hs] = v[:, hs] * beta
        rhs_s[:, GDN_W + h * GDN_HD:GDN_W + (h + 1) * GDN_HD] = kb * eg
        gcs_s[:, hs] = jnp.broadcast_to(col(gc_x, X_GA), (tile, GDN_HD))
        gl_s[:, hs] = jnp.broadcast_to(col(el_x, X_GA), (tile, GDN_HD))

    same, rr, cc = _same_block(STACK, STACK, CHUNK)
    strict = same & (rr > cc)
    incl = same & (rr >= cc)
    lane = _iota2((STACK, GDN_HD), 1)
    pick3 = jnp.where(lane < 3, 1.0, 0.0).astype(BF16)
    row_head = [jnp.where((_iota2((1, STACK), 1) // CHUNK) == h, 1.0, 0.0).astype(F32) for h in range(GDN_HEADS)]

    def stack(x):
        return jnp.concatenate([x[:, h * GDN_HD:(h + 1) * GDN_HD] for h in range(GDN_HEADS)], axis=0)

    def unstack(y):
        return jnp.concatenate([y[h * CHUNK:(h + 1) * CHUNK] for h in range(GDN_HEADS)], axis=1)

    def per_head(fn):
        return [fn(h, slice(h * CHUNK, (h + 1) * CHUNK)) for h in range(GDN_HEADS)]

    def decay_mask(r):
        gcol = stack(gcs_s[r, :])
        col_form = jnp.concatenate([gcol, gcol], axis=1)
        g1, g2, g3 = _split3(gcol)
        pieces = jnp.where(lane == 0, g1, jnp.where(lane == 1, g2, jnp.where(lane == 2, g3, jnp.zeros_like(g1))))
        row_form = lax.dot_general(pick3, pieces, (((1,), (1,)), ((), ())), preferred_element_type=F32)
        return jnp.exp(jnp.where(incl, col_form - row_form, NEG_BIG))

    chunks = range(tile // CHUNK)
    rows = [slice(c * CHUNK, (c + 1) * CHUNK) for c in chunks]
    k_c = [stack(k_s[r, :]).astype(BF16) for r in rows]
    s_kb = [_mm_nt(stack(kb_s[r, :]), k_c[c]) for c, r in enumerate(rows)]
    s_q = [_mm_nt(stack(q_s[r, :]), k_c[c]) for c, r in enumerate(rows)]
    decay = [decay_mask(r) for r in rows]
    attn = [jnp.where(incl, s_q[c] * decay[c], 0.0).astype(BF16) for c in chunks]
    t_inv = _unit_lower_inverses([jnp.where(strict, s_kb[c] * decay[c], 0.0) for c in chunks])
    sol = [_mm(t_inv[c], jnp.concatenate([stack(rhs_s[r, 0:GDN_W]), stack(rhs_s[r, GDN_W:2 * GDN_W])], axis=1))
           for c, r in enumerate(rows)]

    states = [state_ref[h] for h in range(GDN_HEADS)]
    for c in chunks:
        r = rows[c]
        u_c, w_c = sol[c][:, 0:GDN_HD], sol[c][:, GDN_HD:2 * GDN_HD]
        qd_c = stack(qd_s[r, :])
        kd_t = stack(kd_s[r, :]).T
        v_new = u_c - jnp.concatenate(per_head(lambda h, hr: _mm(w_c[hr], states[h])), axis=0)
        o_c = jnp.concatenate(per_head(lambda h, hr: _mm(qd_c[hr], states[h])), axis=0) + _mm(attn[c], v_new)
        o_s[r, :] = unstack(o_c)
        gl = gl_s[c * CHUNK:c * CHUNK + 1, :]
        states = [states[h] * gl[:, h * GDN_HD:(h + 1) * GDN_HD] + _mm(kd_t * row_head[h], v_new)
                  for h in range(GDN_HEADS)]
    for h in range(GDN_HEADS):
        state_ref[h] = states[h]

    o = o_s[...]
    ms = _head_sums(o * o, GDN_HD) * (1.0 / GDN_HD)
    z = pb_ref[:, width3:width3 + GDN_W]
    y_ref[...] = o * lax.rsqrt(ms + RMS_EPS) * norm_g * _silu(z)


def _gdn(pb, px, conv_w, vec, vec_x):
    n = pb.shape[0]
    tile = MIX_TILE
    wide = lambda w: pltpu.VMEM((tile, w), F32)
    scratch = [pltpu.VMEM((GDN_HEADS, GDN_HD, GDN_HD), F32), pltpu.VMEM((8, 3 * GDN_W), F32),
               wide(GDN_W), wide(GDN_W), wide(GDN_W), wide(GDN_W), wide(GDN_W), wide(2 * GDN_W),
               wide(GDN_W), wide(GDN_W), wide(GDN_W)]
    return pl.pallas_call(
        _gdn_kernel,
        grid=(n // tile,),
        in_specs=[_row_spec(tile, 2048), _row_spec(tile, 128), _const_spec((GDN_CONV, 3 * GDN_W)),
                  _const_spec((8, GDN_W)), _const_spec((8, 128))],
        out_specs=_row_spec(tile, GDN_W),
        out_shape=jax.ShapeDtypeStruct((n, GDN_W), F32),
        scratch_shapes=scratch,
        compiler_params=_params(),
        name="gdn",
    )(pb, px, conv_w, vec, vec_x)


def _ret_kernel(pc_ref, pos_ref, freq_ref, lg_ref, lgs_ref, vec_ref, y_ref,
                state_ref, dmat_ref, xi_ref, zeta_ref, q_s, k_s, v_s, o_s):
    n_stack = RET_HEADS * RET_CHUNK

    @pl.when(pl.program_id(0) == 0)
    def _():
        state_ref[...] = jnp.zeros_like(state_ref)
        same, r, c = _same_block(n_stack, n_stack, RET_CHUNK)
        dist = (r - c).astype(F32)
        dmat_ref[...] = jnp.exp(jnp.where(same & (r >= c), dist * lgs_ref[...], NEG_BIG))
        idx = (_iota2((n_stack, RET_W), 0) % RET_CHUNK).astype(F32)
        xi_ref[...] = jnp.exp((idx + 1.0) * lg_ref[...])
        zeta_ref[...] = jnp.exp((RET_CHUNK - 1.0 - idx) * lg_ref[...])

    tile = pc_ref.shape[0]
    ln_g, ln_b = vec_ref[0:1, :], vec_ref[1:2, :]
    first_half = (_iota2((1, 128), 1) % RET_HD) < RET_HD // 2
    ang = pos_ref[...].astype(F32) * freq_ref[:, 0:128]
    cos_slab = jnp.cos(ang)
    sin_slab = jnp.where(first_half, -1.0, 1.0) * jnp.sin(ang)
    cos = jnp.concatenate([cos_slab] * (RET_W // 128), axis=1)
    sin_signed = jnp.concatenate([sin_slab] * (RET_W // 128), axis=1)

    def partner(t):
        slabs = [t[:, j * 128:(j + 1) * 128] for j in range(RET_W // 128)]
        return jnp.concatenate([jnp.where(first_half, pltpu.roll(sl, 128 - RET_HD // 2, axis=1),
                                          pltpu.roll(sl, RET_HD // 2, axis=1)) for sl in slabs], axis=1)

    def rotary(t):
        return t * cos + partner(t) * sin_signed

    q_s[...] = rotary(pc_ref[:, 0:RET_W])
    k_s[...] = rotary(pc_ref[:, RET_W:2 * RET_W]) * (RET_HD ** -0.5)
    v_s[...] = pc_ref[:, 2 * RET_W:3 * RET_W]
    gate = pc_ref[:, 3 * RET_W:4 * RET_W]
    masks = _head_masks(RET_W, RET_HD)
    chunk_decay = jnp.exp(float(RET_CHUNK) * lg_ref[...])

    def chunk_body(c, carry):
        rows = pl.ds(pl.multiple_of(c * RET_CHUNK, RET_CHUNK), RET_CHUNK)
        q4 = _stack_heads(q_s[rows, :], masks)
        k4 = _stack_heads(k_s[rows, :], masks)
        v4 = _stack_heads(v_s[rows, :], masks)
        scores = _mm_nt(q4, k4) * dmat_ref[...]
        state = state_ref[...]
        o4 = _mm(scores, v4) + _mm(q4 * xi_ref[...], state)
        o_s[rows, :] = _fold_heads(o4, RET_HEADS)
        state_ref[...] = state * chunk_decay + _mm((k4 * zeta_ref[...]).T, v4)
        return carry

    lax.fori_loop(0, tile // RET_CHUNK, chunk_body, 0)

    o = o_s[...]
    inv_n = 1.0 / RET_HD
    mu = _head_sums(o, RET_HD) * inv_n
    oc = o - mu
    var = _head_sums(oc * oc, RET_HD) * inv_n
    y_ref[...] = (oc * lax.rsqrt(var + LN_EPS) * ln_g + ln_b) * _silu(gate)


def _ret(pc, pos_col, freq_full, lg_full, lg_stack, vec):
    n = pc.shape[0]
    tile = MIX_TILE
    n_stack = RET_HEADS * RET_CHUNK
    scratch = [pltpu.VMEM((RET_W, RET_W), F32), pltpu.VMEM((n_stack, n_stack), F32),
               pltpu.VMEM((n_stack, RET_W), F32), pltpu.VMEM((n_stack, RET_W), F32)]
    scratch += [pltpu.VMEM((tile, RET_W), F32) for _ in range(4)]
    return pl.pallas_call(
        _ret_kernel,
        grid=(n // tile,),
        in_specs=[_row_spec(tile, 1024), _row_spec(tile, 1), _const_spec((1, RET_W)), _const_spec((1, RET_W)),
                  _const_spec((1, n_stack)), _const_spec((8, RET_W))],
        out_specs=_row_spec(tile, RET_W),
        out_shape=jax.ShapeDtypeStruct((n, RET_W), F32),
        scratch_shapes=scratch,
        compiler_params=_params(),
        name="retention",
    )(pc, pos_col, freq_full, lg_full, lg_stack, vec)


ROW_SUB = D_MODEL // 128


def _row_copy(src, src_row, dst, dst_row, sem):
    s0 = pl.multiple_of(src_row * ROW_SUB, ROW_SUB)
    d0 = pl.multiple_of(dst_row * ROW_SUB, ROW_SUB)
    return pltpu.make_async_copy(src.at[pl.ds(s0, ROW_SUB), :], dst.at[pl.ds(d0, ROW_SUB), :], sem)


def _to_row_tiles(ref, x):
    rows = x.shape[0]
    for s in range(ROW_SUB):
        ref[pl.ds(s, rows, stride=ROW_SUB), :] = x[:, s * 128:(s + 1) * 128]


def _from_row_tiles(ref, rows):
    return jnp.concatenate([ref[pl.ds(s, rows, stride=ROW_SUB), :] for s in range(ROW_SUB)], axis=1)


def _outproj_kernel(alpha, ya_ref, yb_ref, yc_ref, x_ref, w_ref, gt_ref, g_ref, b_ref, o_ref, ot_ref):
    mix = (_mm(ya_ref[...], w_ref[0:256, :]) + _mm(yb_ref[...], w_ref[256:768, :])
           + _mm(yc_ref[...], w_ref[768:1024, :]))
    x1 = _layer_norm_rows(alpha * x_ref[...] + (1.0 + gt_ref[...]) * mix, g_ref[...], b_ref[...])
    o_ref[...] = x1
    _to_row_tiles(ot_ref, x1)


def _outproj(alpha, ya, yb, yc, x2, w_out, gt, ln_g, ln_b):
    n, d = x2.shape
    tile = PROJ_TILE
    vecs = _const_spec((1, d))
    return pl.pallas_call(
        functools.partial(_outproj_kernel, alpha),
        grid=(n // tile,),
        in_specs=[_row_spec(tile, 256), _row_spec(tile, 512), _row_spec(tile, 256), _row_spec(tile, d),
                  _const_spec((d, d)), vecs, vecs, vecs],
        out_specs=[_row_spec(tile, d), _row_spec(tile * ROW_SUB, 128)],
        out_shape=[jax.ShapeDtypeStruct((n, d), F32), jax.ShapeDtypeStruct((n * ROW_SUB, 128), F32)],
        compiler_params=_params(),
        name="outproj_ln",
    )(ya, yb, yc, x2, w_out, gt, ln_g, ln_b)


def _router_kernel(x_ref, sc_ref, sh_ref, wt_ref, b_ref, idx_ref, rank_ref, gate_ref, cnt_ref, run_ref):
    @pl.when(pl.program_id(0) == 0)
    def _():
        run_ref[...] = jnp.zeros_like(run_ref)

    tile = x_ref.shape[0]
    h = x_ref[...] * (1.0 + sc_ref[...]) + sh_ref[...]
    logits = _mm3_nt(wt_ref[...], h) + b_ref[:, 0:1]
    e_iota = _iota2((N_EXPERTS, tile), 0).astype(F32)
    vals, hots, idxs = [], [], []
    for _ in range(TOP_K):
        m = jnp.max(logits, axis=0, keepdims=True)
        idx = jnp.min(jnp.where(logits == m, e_iota, float(N_EXPERTS)), axis=0, keepdims=True)
        hot = e_iota == idx
        logits = jnp.where(hot, -jnp.inf, logits)
        vals.append(m)
        idxs.append(idx.astype(I32))
        hots.append(hot)
    exps = [jnp.exp(vk - vals[0]) for vk in vals]
    denom = exps[0] + exps[1] + exps[2] + exps[3]
    hot_all = jnp.where(hots[0] | hots[1] | hots[2] | hots[3], 1.0, 0.0)
    before = jnp.where(_iota2((tile, tile), 0) < _iota2((tile, tile), 1), 1.0, 0.0).astype(BF16)
    seen = _dot(hot_all.astype(BF16), before) + run_ref[:, 0:1]
    for kk in range(TOP_K):
        idx_ref[kk:kk + 1, :] = idxs[kk]
        rank_ref[kk:kk + 1, :] = jnp.sum(jnp.where(hots[kk], seen, 0.0), axis=0, keepdims=True).astype(I32)
        gate_ref[kk:kk + 1, :] = exps[kk] / denom
    run_ref[...] = run_ref[...] + jnp.sum(hot_all, axis=1, keepdims=True)
    cnt_ref[...] = run_ref[...].astype(I32)


def _router(x1, sc, sh, router_wt, router_b):
    n, d = x1.shape
    tile = PROJ_TILE
    lane_spec = pl.BlockSpec((TOP_K, tile), lambda i: (0, i))
    return pl.pallas_call(
        _router_kernel,
        grid=(n // tile,),
        in_specs=[_row_spec(tile, d), _const_spec((1, d)), _const_spec((1, d)),
                  _const_spec((N_EXPERTS, d)), _const_spec((N_EXPERTS, 128))],
        out_specs=[lane_spec, lane_spec, lane_spec, _const_spec((N_EXPERTS, 128))],
        out_shape=[jax.ShapeDtypeStruct((TOP_K, n), I32), jax.ShapeDtypeStruct((TOP_K, n), I32),
                   jax.ShapeDtypeStruct((TOP_K, n), F32), jax.ShapeDtypeStruct((N_EXPERTS, 128), I32)],
        scratch_shapes=[pltpu.VMEM((N_EXPERTS, 128), F32)],
        compiler_params=_params(),
        name="router",
    )(x1, sc, sh, router_wt, router_b)


def _dest_kernel(pstart_ref, idx_ref, rank_ref, dest_ref):
    idx = idx_ref[...]
    dest = rank_ref[...]
    for e in range(N_EXPERTS):
        dest = dest + jnp.where(idx == e, pstart_ref[e], 0)
    dest_ref[...] = dest


def _dest(pstart, idx, rank):
    grid_spec = pltpu.PrefetchScalarGridSpec(
        num_scalar_prefetch=1, grid=(1,),
        in_specs=[pl.BlockSpec(idx.shape, lambda i, *_: (0, 0)), pl.BlockSpec(idx.shape, lambda i, *_: (0, 0))],
        out_specs=pl.BlockSpec(idx.shape, lambda i, *_: (0, 0)))
    return pl.pallas_call(_dest_kernel, grid_spec=grid_spec, out_shape=jax.ShapeDtypeStruct(idx.shape, I32),
                          compiler_params=_params(), name="moe_dest")(pstart, idx, rank)


def _rowmap_kernel(first, count, dest_ref, in_ref, out_ref, map_s, sem):
    load = pltpu.make_async_copy(in_ref, map_s, sem)
    load.start()
    load.wait()

    unroll = 8

    def scatter(i, c):
        rows = [dest_ref[first + i * unroll + j] for j in range(unroll)]
        for j in range(unroll):
            map_s[rows[j]] = first + i * unroll + j
        return c

    lax.fori_loop(0, count // unroll, scatter, 0)
    store = pltpu.make_async_copy(map_s, out_ref, sem)
    store.start()
    store.wait()


def _rowmap(dest_flat, cap):
    n_assign = dest_flat.shape[0]
    rows = jnp.arange(cap, dtype=I32)
    rowmap = n_assign + (rows // MOE_ROWS) % MOE_RING * MOE_ROWS + rows % MOE_ROWS
    halves = 2
    count = n_assign // halves
    for part in range(halves):
        grid_spec = pltpu.PrefetchScalarGridSpec(
            num_scalar_prefetch=1, grid=(1,), in_specs=[pl.BlockSpec(memory_space=pl.ANY)],
            out_specs=pl.BlockSpec(memory_space=pl.ANY),
            scratch_shapes=[pltpu.SMEM((cap,), I32), pltpu.SemaphoreType.DMA])
        rowmap = pl.pallas_call(functools.partial(_rowmap_kernel, part * count, count), grid_spec=grid_spec,
                                out_shape=jax.ShapeDtypeStruct((cap,), I32), compiler_params=_params(),
                                name="moe_rowmap")(dest_flat, rowmap)
    return rowmap


def _expert_kernel(n_tok, be_ref, map_ref, x_ref, sc_ref, sh_ref, wu_ref, bu_ref, wd_ref, bd_ref, out_ref,
                   xbuf, ybuf, wu_s, wd_s, gsem, ssem):
    b = pl.program_id(0)
    nb = pl.num_programs(0)
    n_assign = TOP_K * n_tok
    block_rows = MOE_ROWS * ROW_SUB
    slot = lax.rem(b, MOE_RING)
    slot_prev = lax.rem(b + MOE_RING - 1, MOE_RING)
    slot_next = lax.rem(b + 1, MOE_RING)

    def gather(block, j, into):
        m = map_ref[block * MOE_ROWS + j]
        tok = (m & (n_tok - 1)) if n_tok & (n_tok - 1) == 0 else lax.rem(m, n_tok)
        return _row_copy(x_ref, tok, xbuf.at[into], j, gsem.at[into])

    def scatter(block, j, frm, warm_up=None):
        m = map_ref[jnp.maximum(block, 0) * MOE_ROWS + j]
        dst = m if warm_up is None else jnp.where(warm_up, n_assign + (MOE_RING - 1) * MOE_ROWS + j, m)
        return _row_copy(ybuf.at[frm], j, out_ref, dst, ssem.at[frm])

    def gather_wait(into):
        pltpu.make_async_copy(x_ref.at[pl.ds(0, block_rows), :], xbuf.at[into], gsem.at[into]).wait()

    def scatter_wait(frm):
        pltpu.make_async_copy(ybuf.at[frm], out_ref.at[pl.ds(0, block_rows), :], ssem.at[frm]).wait()

    def for_rows(fn):
        for j in range(MOE_ROWS):
            fn(j)

    @pl.when(b == 0)
    def _():
        ybuf[MOE_RING - 1] = jnp.zeros(ybuf.shape[1:], F32)
        for r in range(MOE_RING - 1):
            clear = pltpu.make_async_copy(
                ybuf.at[MOE_RING - 1], out_ref.at[pl.ds((n_assign + r * MOE_ROWS) * ROW_SUB, block_rows), :],
                ssem.at[r])
            clear.start()
            clear.wait()
        for_rows(lambda j: gather(0, j, 0).start(priority=j % 2))
        for_rows(lambda j: gather(jnp.minimum(1, nb - 1), j, 1).start(priority=j % 2))

    gather_wait(slot)

    @pl.when(b >= 2)
    def _():
        scatter_wait(slot)

    @pl.when((b == 0) | (be_ref[b] != be_ref[jnp.maximum(b - 1, 0)]))
    def _():
        wu_s[...] = wu_ref[0, 0].astype(BF16)
        wd_s[...] = wd_ref[0, 0].astype(BF16)

    warm_up = b == 0
    for_rows(lambda j: scatter(b - 1, j, slot_prev, warm_up).start(priority=j % 2))
    h = (_from_row_tiles(xbuf.at[slot], MOE_ROWS) * (1.0 + sc_ref[...]) + sh_ref[...]).astype(BF16)
    ahead = jnp.minimum(b + 2, nb - 1)
    for_rows(lambda j: gather(ahead, j, slot_prev).start(priority=j % 2))
    up = _dot(h, wu_s[...]) + bu_ref[0, 0]
    glu = jnp.minimum(up[:, 0:D_FF], SWIGLU_LIMIT)
    lin = jnp.clip(up[:, D_FF:2 * D_FF], -SWIGLU_LIMIT, SWIGLU_LIMIT)
    act = glu * _sigmoid(SWIGLU_ALPHA * glu) * (lin + 1.0)
    _to_row_tiles(ybuf.at[slot], _dot(act.astype(BF16), wd_s[...]) + bd_ref[0, 0])

    @pl.when(b == nb - 1)
    def _():
        gather_wait(slot_next)
        gather_wait(slot_prev)
        for_rows(lambda j: scatter(b, j, slot).start(priority=j % 2))
        if MOE_RING > 2:
            scatter_wait(slot_next)
        scatter_wait(slot_prev)
        scatter_wait(slot)


def _experts(layer, block_e, rowmap, x1t, sc, sh, w_up, b_up, w_down, b_down):
    d = D_MODEL
    depth = w_up.shape[0]
    n_tok = x1t.shape[0] // ROW_SUB
    cap = rowmap.shape[0]
    n_blocks = cap // MOE_ROWS
    assert n_blocks >= MOE_RING

    def wmap(b, be, mp):
        return (layer, be[b], 0, 0)

    vec = pl.BlockSpec((1, d), lambda b, be, mp: (0, 0))
    grid_spec = pltpu.PrefetchScalarGridSpec(
        num_scalar_prefetch=2,
        grid=(n_blocks,),
        in_specs=[pl.BlockSpec(memory_space=pl.ANY), vec, vec,
                  pl.BlockSpec((1, 1, d, 2 * D_FF), wmap), pl.BlockSpec((1, 1, 1, 2 * D_FF), wmap),
                  pl.BlockSpec((1, 1, D_FF, d), wmap), pl.BlockSpec((1, 1, 1, d), wmap)],
        out_specs=pl.BlockSpec(memory_space=pl.ANY),
        scratch_shapes=[pltpu.VMEM((MOE_RING, MOE_ROWS * ROW_SUB, 128), F32),
                        pltpu.VMEM((MOE_RING, MOE_ROWS * ROW_SUB, 128), F32),
                        pltpu.VMEM((d, 2 * D_FF), BF16), pltpu.VMEM((D_FF, d), BF16),
                        pltpu.SemaphoreType.DMA((MOE_RING,)), pltpu.SemaphoreType.DMA((MOE_RING,))],
    )
    return pl.pallas_call(
        functools.partial(_expert_kernel, n_tok),
        grid_spec=grid_spec,
        out_shape=jax.ShapeDtypeStruct(((TOP_K * n_tok + MOE_RING * MOE_ROWS) * ROW_SUB, 128), F32),
        compiler_params=_params(),
        name="moe_experts",
    )(block_e, rowmap, x1t, sc, sh, w_up, b_up.reshape(depth, N_EXPERTS, 1, 2 * D_FF), w_down,
      b_down.reshape(depth, N_EXPERTS, 1, d))


def _combine_kernel(alpha, gates_ref, y0_ref, y1_ref, y2_ref, y3_ref, x_ref, gt_ref, g_ref, b_ref, o_ref):
    tile = x_ref.shape[0]
    eye = jnp.where(_iota2((tile, tile), 0) == _iota2((tile, tile), 1), 1.0, 0.0).astype(BF16)
    gates = _sel_mm_nt(eye, jnp.concatenate([gates_ref[...], jnp.zeros((8 - TOP_K, tile), F32)], axis=0))
    ffn = _from_row_tiles(y0_ref, tile) * gates[:, 0:1]
    for kk, y_ref in enumerate((y1_ref, y2_ref, y3_ref), start=1):
        ffn = ffn + _from_row_tiles(y_ref, tile) * gates[:, kk:kk + 1]
    o_ref[...] = _layer_norm_rows(alpha * x_ref[...] + (1.0 + gt_ref[...]) * ffn, g_ref[...], b_ref[...])


def _combine(alpha, out4, gates, x1, gt, ln_g, ln_b):
    n, d = x1.shape
    tile = MOE_TILE
    steps = n // tile
    vec = _const_spec((1, d))
    y_specs = [pl.BlockSpec((tile * ROW_SUB, 128), functools.partial(lambda k, i: (k * steps + i, 0), k))
               for k in range(TOP_K)]
    return pl.pallas_call(
        functools.partial(_combine_kernel, alpha),
        grid=(steps,),
        in_specs=[pl.BlockSpec((TOP_K, tile), lambda i: (0, i))] + y_specs + [_row_spec(tile, d), vec, vec, vec],
        out_specs=_row_spec(tile, d),
        out_shape=jax.ShapeDtypeStruct((n, d), F32),
        compiler_params=_params(),
        name="moe_combine_ln",
    )(gates, out4, out4, out4, out4, x1, gt, ln_g, ln_b)


def _pad_rows(w, rows, offset=0):
    out = jnp.zeros((rows, w.shape[1]), w.dtype)
    return out.at[offset:offset + w.shape[0]].set(w)


def _moe(layer, alpha, x1, x1t, sc, sh, gt, router_w, router_b, w_up, b_up, w_down, b_down, ln_g, ln_b):
    n, d = x1.shape
    router_bias = jnp.broadcast_to(router_b[:, None], (N_EXPERTS, 128))
    idx, rank, gates, counts = _router(x1, sc, sh, router_w.T, router_bias)
    counts = counts[:, 0]
    padded = (counts + MOE_ROWS - 1) // MOE_ROWS * MOE_ROWS
    pend = jnp.cumsum(padded).astype(I32)
    pstart = pend - padded
    dest = _dest(pstart, idx, rank).reshape(TOP_K * n)
    n_blocks = -(-(n * TOP_K + N_EXPERTS * (MOE_ROWS - 1)) // MOE_ROWS)
    block_row = jnp.arange(n_blocks, dtype=I32)[:, None] * MOE_ROWS
    block_e = jnp.minimum(jnp.sum((pend[None, :] <= block_row).astype(I32), axis=1), N_EXPERTS - 1)
    rowmap = _rowmap(dest, n_blocks * MOE_ROWS)
    out4 = _experts(layer, block_e, rowmap, x1t, sc, sh, w_up, b_up, w_down, b_down)
    return _combine(alpha, out4, gates, x1, gt, ln_g, ln_b)


def kernel(x, c, positions, ada_w, ada_b, w_in, w_in_vres, tshift_mu, tshift_mu_vres, rwkv_w0, rwkv_w2, rwkv_a0, rwkv_a2, rwkv_g2, rwkv_kk, rwkv_ka, rwkv_rk, rwkv_ln_g, rwkv_ln_b, rwkv_v0, rwkv_v2, gdn_conv_w, gdn_a_log, gdn_dt_bias, gdn_norm_g, ret_norm_g, ret_norm_b, w_out, ln1_g, ln1_b, router_w, router_b, exp_w_up, exp_b_up, exp_w_down, exp_b_down, ln2_g, ln2_b):
    bsz, seq, d = x.shape
    assert bsz == 1 and d == D_MODEL and seq % MIX_TILE == 0
    depth = ada_w.shape[0]
    alpha = (2 * depth) ** 0.25
    n = seq
    x2 = x.reshape(n, d)
    ada = _ada(c, ada_w, ada_b)

    half = RET_HD // 2
    inv_freq = ROPE_BASE ** (-jnp.arange(half, dtype=F32) / half)
    freq_full = jnp.tile(inv_freq, 2 * RET_HEADS)[None, :]
    log_gamma = jnp.log1p(-jnp.exp2(-5.0 - jnp.arange(RET_HEADS, dtype=F32)))
    lg_full = jnp.repeat(log_gamma, RET_HD)[None, :]
    lg_stack = jnp.repeat(log_gamma, RET_CHUNK)[None, :]
    pos_col = positions.reshape(n, 1)

    v_first = None
    for l in range(depth):
        sh_mix, sc_mix, gt_mix, sh_ffn, sc_ffn, gt_ffn = (ada[l, :, i * d:(i + 1) * d] for i in range(6))
        w_l = w_in[l]
        zeros32 = jnp.zeros((d, 32), F32)
        vres_w = zeros32 if l == 0 else w_in_vres[l - 1]
        x_cols = jnp.concatenate([vres_w, w_l[:, 3072:3080], jnp.zeros((d, 88), F32)], axis=1)
        w_cat = jnp.concatenate([w_l[:, 0:1024], w_l[:, 1024:3072], w_l[:, 3080:4104], x_cols], axis=1).astype(BF16)
        mu_a = tshift_mu[l][None, :]
        mu_v = jnp.zeros((32,), F32) if l == 0 else tshift_mu_vres[l - 1]
        mu_x = jnp.concatenate([mu_v, jnp.zeros((96,), F32)])[None, :]
        pa, pb, pc, px = _inproj(x2, sc_mix, sh_mix, w_cat, mu_a, mu_x)

        v0 = jnp.zeros((RWKV_W,), F32) if l == 0 else rwkv_v0[l - 1]
        vec_a = jnp.stack([rwkv_w0[l], rwkv_a0[l], rwkv_kk[l], rwkv_ka[l], rwkv_rk[l].reshape(RWKV_W),
                           rwkv_ln_g[l], rwkv_ln_b[l], v0])
        w2p = _pad_rows(rwkv_w2[l], 128, 0).astype(BF16)
        a2p = _pad_rows(rwkv_a2[l], 128, 64).astype(BF16)
        g2 = rwkv_g2[l].astype(BF16)
        if l == 0:
            y_a, v_first = _rwkv(pa, None, None, vec_a, w2p, a2p, g2, None)
        else:
            v2p = _pad_rows(rwkv_v2[l - 1], 128, X_VRES).astype(BF16)
            y_a = _rwkv(pa, px, v_first, vec_a, w2p, a2p, g2, v2p)

        vec_b = jnp.concatenate([jnp.tile(gdn_norm_g[l], GDN_HEADS)[None, :], jnp.zeros((7, GDN_W), F32)])
        lane_pad = lambda t: jnp.concatenate([jnp.zeros((X_GA,), F32), t, jnp.zeros((128 - X_GA - GDN_HEADS,), F32)])
        vec_x = jnp.concatenate([jnp.stack([lane_pad(gdn_a_log[l]), lane_pad(gdn_dt_bias[l])]),
                                 jnp.zeros((6, 128), F32)])
        y_b = _gdn(pb, px, gdn_conv_w[l], vec_b, vec_x)

        vec_c = jnp.concatenate([jnp.stack([ret_norm_g[l], ret_norm_b[l]]), jnp.zeros((6, RET_W), F32)])
        y_c = _ret(pc, pos_col, freq_full, lg_full, lg_stack, vec_c)

        x1, x1t = _outproj(alpha, y_a, y_b, y_c, x2, w_out[l].astype(BF16), gt_mix, ln1_g[l][None, :],
                           ln1_b[l][None, :])
        x2 = _moe(l, alpha, x1, x1t, sc_ffn, sh_ffn, gt_ffn, router_w[l], router_b[l], exp_w_up, exp_b_up,
                  exp_w_down, exp_b_down, ln2_g[l][None, :], ln2_b[l][None, :])
    return x2.reshape(bsz, seq, d)
```
